```python
import math
import jax, jax.numpy as jnp
from jax import lax
import numpy as np

D_MODEL = 1024
BATCH = 8
SEQ = 8192
DEPTH = 4

CHUNK = 64
N_MIXERS = 3
NORM_EPS = 1e-6
N_SUBLAYER_NORMS = 6

HG_EXPAND = 128
HG_HEADS = D_MODEL // HG_EXPAND
HG_F_DIM = HG_HEADS * HG_EXPAND
HG_V_DIM = D_MODEL // HG_HEADS
HG_BLOCK = 16
HG_MID = HG_BLOCK // 2

DA_HEAD_DIM = 64
DA_HEADS = D_MODEL // (2 * DA_HEAD_DIM)
DA_Q_BLOCK = 128

POOL_WINDOWS = (2, 4, 8, 16)
POOL_GROUPS = len(POOL_WINDOWS)
POOL_GROUP_DIM = D_MODEL // POOL_GROUPS

D_FF = 2816

N_A = (DEPTH + 2) // 3
N_B = (DEPTH + 1) // 3
N_C = DEPTH // 3

kernel_name = "hybrid_hgrn2_diffattn_pool_macaron"


def rmsnorm(x, w):
    xf = x.astype(jnp.float32)
    y = xf * lax.rsqrt(jnp.mean(xf * xf, axis=-1, keepdims=True) + NORM_EPS)
    return (y * w.astype(jnp.float32)).astype(x.dtype)


def swiglu(x, w_in, w_out):
    g, u = jnp.split(x @ w_in, 2, axis=-1)
    return (jax.nn.silu(g) * u) @ w_out


def alibi_slopes(n_heads):
    return jnp.exp2(-8.0 * jnp.arange(1, n_heads + 1, dtype=jnp.float32) / n_heads)


def hgrn_lower_bound(lb_raw, layer):
    p = jax.nn.softmax(lb_raw.astype(jnp.float32), axis=0)
    return jnp.cumsum(p, axis=0)[layer] - p[0]


def hgrn2_mixer(x, w_in, gnorm_w, w_out, lb):
    f32 = jnp.float32
    B, S, _ = x.shape
    nb = S // HG_BLOCK
    q, f_pre, i_in, g = jnp.split(
        x @ w_in, [HG_F_DIM, 2 * HG_F_DIM, 2 * HG_F_DIM + D_MODEL], axis=-1)
    f = lb + (1.0 - lb) * jax.nn.sigmoid(f_pre.astype(f32))
    k = 1.0 - f
    log_f = jnp.log(f)

    def heads(t, dim):
        return t.reshape(B, nb, HG_BLOCK, HG_HEADS, dim).transpose(0, 3, 1, 2, 4)

    qh = heads(q.astype(f32), HG_EXPAND)
    kh = heads(k, HG_EXPAND)
    vh = heads(i_in.astype(f32), HG_V_DIM)
    b = jnp.cumsum(heads(log_f, HG_EXPAND), axis=3)
    b_last = b[:, :, :, -1:, :]
    b_mid = b[:, :, :, HG_MID:HG_MID + 1, :]

    q_r = qh * jnp.exp(b - b_mid)
    k_r = kh * jnp.exp(b_mid - b)
    causal = jnp.tril(jnp.ones((HG_BLOCK, HG_BLOCK), dtype=bool))
    a = jnp.where(causal, jnp.einsum('bhntk,bhnsk->bhnts', q_r, k_r), 0.0)
    o_intra = jnp.einsum('bhnts,bhnsv->bhntv', a, vh)

    q_inter = qh * jnp.exp(b)
    k_state = kh * jnp.exp(b_last - b)
    decay = jnp.exp(b_last[:, :, :, 0, :])

    def step(state, xs):
        qi, ks, vs, dec = xs
        o = jnp.einsum('bhtk,bhkv->bhtv', qi, state)
        state = state * dec[..., None] + jnp.einsum('bhtk,bhtv->bhkv', ks, vs)
        return state, o

    xs = (jnp.moveaxis(q_inter, 2, 0), jnp.moveaxis(k_state, 2, 0),
          jnp.moveaxis(vh, 2, 0), jnp.moveaxis(decay, 2, 0))
    s0 = jnp.zeros((B, HG_HEADS, HG_EXPAND, HG_V_DIM), f32)
    _, o_inter = lax.scan(step, s0, xs)

    o = o_intra + jnp.moveaxis(o_inter, 0, 2)
    o = o.transpose(0, 2, 3, 1, 4).reshape(B, S, HG_HEADS, HG_V_DIM)
    gate = jax.nn.silu(g.astype(f32).reshape(B, S, HG_HEADS, HG_V_DIM))
    o = rmsnorm(o, gnorm_w) * gate
    return o.reshape(B, S, D_MODEL).astype(x.dtype) @ w_out


def diff_attention(x, w_in, lam, subln_w, w_out, layer):
    f32 = jnp.float32
    B, S, _ = x.shape
    H, dh = DA_HEADS, DA_HEAD_DIM
    q, k, v = jnp.split(x @ w_in, 3, axis=-1)
    q = q.reshape(B, S, H, 2, dh).transpose(0, 2, 3, 1, 4)
    k = k.reshape(B, S, H, 2, dh).transpose(0, 2, 3, 1, 4)
    v = v.reshape(B, S, H, 2 * dh).transpose(0, 2, 1, 3)

    lam_init = 0.8 - 0.6 * math.exp(-0.3 * layer)
    lam_f = lam.astype(f32)
    lam_full = (jnp.exp(jnp.sum(lam_f[0] * lam_f[1]))
                - jnp.exp(jnp.sum(lam_f[2] * lam_f[3])) + lam_init)
    slopes = alibi_slopes(H)
    scale = dh ** -0.5
    n_qb = S // DA_Q_BLOCK
    q_blocks = q.reshape(B, H, 2, n_qb, DA_Q_BLOCK, dh).transpose(3, 0, 1, 2, 4, 5)
    key_pos = jnp.arange(S)

    def block(args):
        qb, start = args
        q_pos = start + jnp.arange(DA_Q_BLOCK)
        s = jnp.einsum('bhgqd,bhgkd->bhgqk', qb, k).astype(f32) * scale
        dist = jnp.abs(q_pos[:, None] - key_pos[None, :]).astype(f32)
        allowed = (key_pos[None, :] // CHUNK) <= (q_pos[:, None] // CHUNK)
        s = s - slopes[None, :, None, None, None] * dist
        p = jax.nn.softmax(jnp.where(allowed, s, -jnp.inf), axis=-1)
        w = p[:, :, 0] - lam_full * p[:, :, 1]
        return jnp.einsum('bhqk,bhkv->bhqv', w.astype(v.dtype), v)

    starts = jnp.arange(n_qb, dtype=jnp.int32) * DA_Q_BLOCK
    o = lax.map(block, (q_blocks, starts))
    o = o.transpose(1, 0, 3, 2, 4).reshape(B, S, H, 2 * dh)
    o = rmsnorm(o, subln_w) * (1.0 - lam_init)
    return o.reshape(B, S, D_MODEL) @ w_out


def pool_mixer(x, w_in, w_group, scale, w_out):
    f32 = jnp.float32
    B, S, _ = x.shape
    u = (x @ w_in).astype(f32).reshape(B, S, POOL_GROUPS, POOL_GROUP_DIM)
    csum = jnp.cumsum(u, axis=1)
    count_pos = jnp.arange(S)
    outs = []
    for g_idx, w in enumerate(POOL_WINDOWS):
        cs = csum[:, :, g_idx]
        lag = jnp.pad(cs[:, :S - w], ((0, 0), (w, 0), (0, 0)))
        count = jnp.minimum(count_pos + 1, w).astype(f32)
        mean = (cs - lag) / count[None, :, None]
        outs.append(mean - u[:, :, g_idx])
    p = jnp.stack(outs, axis=2)
    y = jnp.einsum('bsgc,gcd->bsgd', p, w_group.astype(f32))
    y = y * scale.astype(f32).reshape(POOL_GROUPS, POOL_GROUP_DIM)
    return y.reshape(B, S, D_MODEL).astype(x.dtype) @ w_out


def setup_inputs(seed: int = 0) -> dict:
    key = jax.random.key(seed)
    ks = jax.random.split(key, 18)
    f32 = jnp.float32

    def dense(k, shape, fan_in):
        return jax.random.normal(k, shape, f32) * fan_in ** -0.5

    def gain(k, shape, s=0.05):
        return 1.0 + s * jax.random.normal(k, shape, f32)

    G, C = POOL_GROUPS, POOL_GROUP_DIM
    return {
        "x": jax.random.normal(ks[0], (BATCH, SEQ, D_MODEL), f32),
        "norm_gains": gain(ks[1], (DEPTH, N_SUBLAYER_NORMS, D_MODEL)),
        "ffn1_w_in": dense(ks[2], (DEPTH, D_MODEL, 2 * D_FF), D_MODEL),
        "ffn1_w_out": dense(ks[3], (DEPTH, D_FF, D_MODEL), D_FF),
        "ffn2_w_in": dense(ks[4], (DEPTH, D_MODEL, 2 * D_FF), D_MODEL),
        "ffn2_w_out": dense(ks[5], (DEPTH, D_FF, D_MODEL), D_FF),
        "hgrn_w_in": dense(ks[6], (N_A, D_MODEL, 2 * HG_F_DIM + 2 * D_MODEL), D_MODEL),
        "hgrn_gnorm": gain(ks[7], (N_A, HG_V_DIM)),
        "hgrn_w_out": dense(ks[8], (N_A, D_MODEL, D_MODEL), D_MODEL),
        "hgrn_lb_raw": 0.5 * jax.random.normal(ks[9], (DEPTH, HG_F_DIM), f32),
        "diff_w_in": dense(ks[10], (N_B, D_MODEL, 3 * D_MODEL), D_MODEL),
        "diff_lambda": 0.1 * jax.random.normal(ks[11], (N_B, 4, DA_HEAD_DIM), f32),
        "diff_subln": gain(ks[12], (N_B, 2 * DA_HEAD_DIM)),
        "diff_w_out": dense(ks[13], (N_B, D_MODEL, D_MODEL), D_MODEL),
        "pool_w_in": dense(ks[14], (N_C, D_MODEL, D_MODEL), D_MODEL),
        "pool_w_group": dense(ks[15], (N_C, G, C, C), C),
        "pool_scale": gain(ks[16], (N_C, D_MODEL), 0.1),
        "pool_w_out": dense(ks[17], (N_C, D_MODEL, D_MODEL), D_MODEL),
    }


def reference(x, norm_gains, ffn1_w_in, ffn1_w_out, ffn2_w_in, ffn2_w_out,
              hgrn_w_in, hgrn_gnorm, hgrn_w_out, hgrn_lb_raw,
              diff_w_in, diff_lambda, diff_subln, diff_w_out,
              pool_w_in, pool_w_group, pool_scale, pool_w_out):
    h = x
    for i in range(DEPTH):
        kind, j = i % N_MIXERS, i // N_MIXERS
        gn = norm_gains[i]
        h = h + 0.5 * rmsnorm(swiglu(rmsnorm(h, gn[0]), ffn1_w_in[i], ffn1_w_out[i]), gn[1])
        hn = rmsnorm(h, gn[2])
        if kind == 0:
            lb = hgrn_lower_bound(hgrn_lb_raw, i)
            m = hgrn2_mixer(hn, hgrn_w_in[j], hgrn_gnorm[j], hgrn_w_out[j], lb)
        elif kind == 1:
            m = diff_attention(hn, diff_w_in[j], diff_lambda[j], diff_subln[j], diff_w_out[j], i)
        else:
            m = pool_mixer(hn, pool_w_in[j], pool_w_group[j], pool_scale[j], pool_w_out[j])
        h = h + rmsnorm(m, gn[3])
        h = h + 0.5 * rmsnorm(swiglu(rmsnorm(h, gn[4]), ffn2_w_in[i], ffn2_w_out[i]), gn[5])
    return h
```

```python
import functools
import math

import jax
import jax.numpy as jnp
from jax import lax
from jax.experimental import pallas as pl
from jax.experimental.pallas import tpu as pltpu

F32 = jnp.float32
BF16 = jnp.bfloat16

NORM_EPS = 1e-6
CHUNK = 64
HG_HEADS = 8
HG_SUB = 16
HG_MID = HG_SUB // 2
DA_HEADS = 8
DA_HEAD_DIM = 64
POOL_WINDOWS = (2, 4, 8, 16)

LANES = 128
V7X_VMEM_BYTES = 64 * 2**20
VMEM_LIMIT = V7X_VMEM_BYTES * 3 // 4

FFN_ROWS = 512
FFN_COLS = 256
MIX_ROWS = 512
HG_CHUNK = 64
ATT_TILE = 512
NEG_BIG = -1e30


def _rmsnorm(x, w):
    ms = jnp.mean(x * x, axis=-1, keepdims=True)
    return x * lax.rsqrt(ms + NORM_EPS) * w


def _dot(a, b):
    return jnp.dot(a, b, preferred_element_type=F32)


def _dot_nt(a, b):
    return lax.dot_general(a, b, (((1,), (1,)), ((), ())), preferred_element_type=F32)


def _dot_tn(a, b):
    return lax.dot_general(a, b, (((0,), (0,)), ((), ())), preferred_element_type=F32)


def _silu(x):
    return x * jax.nn.sigmoid(x)


def _resident(shape):
    zeros = (0,) * len(shape)
    return pl.BlockSpec(shape, lambda *_: zeros, pipeline_mode=pl.Buffered(1))


def _params(n_grid):
    return pltpu.CompilerParams(
        dimension_semantics=("arbitrary",) * n_grid, vmem_limit_bytes=VMEM_LIMIT)


def _ffn_kernel(h_ref, gains_ref, wi_ref, wo_ref, o_ref, act_ref, *, d_ff, pre, post):
    x = h_ref[...]
    xn = _rmsnorm(x, gains_ref[pre:pre + 1, :]).astype(BF16)
    for lo in range(0, d_ff, FFN_COLS):
        g = _dot(xn, wi_ref[:, lo:lo + FFN_COLS])
        u = _dot(xn, wi_ref[:, d_ff + lo:d_ff + lo + FFN_COLS])
        act_ref[:, lo:lo + FFN_COLS] = (_silu(g) * u).astype(BF16)
    y = _dot(act_ref[...], wo_ref[...])
    o_ref[...] = x + 0.5 * _rmsnorm(y, gains_ref[post:post + 1, :])


def _ffn(h2, gains, w_in, w_out, pre, post):
    n, d = h2.shape
    d_ff = w_out.shape[0]
    return pl.pallas_call(
        functools.partial(_ffn_kernel, d_ff=d_ff, pre=pre, post=post),
        grid=(n // FFN_ROWS,),
        in_specs=[pl.BlockSpec((FFN_ROWS, d), lambda i: (i, 0)),
                  _resident(gains.shape), _resident(w_in.shape), _resident(w_out.shape)],
        out_specs=pl.BlockSpec((FFN_ROWS, d), lambda i: (i, 0)),
        out_shape=jax.ShapeDtypeStruct((n, d), F32),
        scratch_shapes=[pltpu.VMEM((FFN_ROWS, d_ff), BF16)],
        compiler_params=_params(1),
        name="ffn",
    )(h2, gains, w_in, w_out)


def _hgrn_kernel(h_ref, gains_ref, wi_ref, gn_ref, wo_ref, lb_ref, o_ref,
                 q_s, k_s, v_s, b_s, gate_s, y_s, st_s, *, layer):
    ts, d = h_ref.shape[1], h_ref.shape[2]
    dh = d // HG_HEADS
    n_sub = HG_CHUNK // HG_SUB

    @pl.when(pl.program_id(1) == 0)
    def _():
        st_s[...] = jnp.zeros_like(st_s)

    x = h_ref[0]
    hn = _rmsnorm(x, gains_ref[2:3, :]).astype(BF16)

    raw = lb_ref[...]
    e = jnp.exp(raw - jnp.max(raw, axis=0, keepdims=True))
    p = e / jnp.sum(e, axis=0, keepdims=True)
    lb = jnp.sum(p[0:layer + 1, :], axis=0, keepdims=True) - p[0:1, :]

    q_s[...] = _dot(hn, wi_ref[:, 0:d])
    f = lb + (1.0 - lb) * jax.nn.sigmoid(_dot(hn, wi_ref[:, d:2 * d]))
    k_s[...] = 1.0 - f
    b = jnp.log(f)
    row_in_chunk = lax.broadcasted_iota(jnp.int32, (ts, d), 0) & (HG_CHUNK - 1)
    shift = 1
    while shift < HG_CHUNK:
        b = b + jnp.where(row_in_chunk >= shift, pltpu.roll(b, shift, axis=0), 0.0)
        shift *= 2
    b_s[...] = b
    v_s[...] = _dot(hn, wi_ref[:, 2 * d:3 * d])
    gate_s[...] = _silu(_dot(hn, wi_ref[:, 3 * d:4 * d]))

    row = lax.broadcasted_iota(jnp.int32, (HG_CHUNK, dh), 0)
    sub_of_row = row >> int(math.log2(HG_SUB))
    tri_r = lax.broadcasted_iota(jnp.int32, (HG_CHUNK, HG_CHUNK), 0)
    tri_c = lax.broadcasted_iota(jnp.int32, (HG_CHUNK, HG_CHUNK), 1)
    causal = tri_c <= tri_r
    gn = gn_ref[...]

    def chunk_body(c, carry):
        r0 = pl.multiple_of(c * HG_CHUNK, HG_CHUNK)
        rows = pl.ds(r0, HG_CHUNK)
        for hd in range(HG_HEADS):
            cols = slice(hd * dh, (hd + 1) * dh)
            q = q_s[rows, cols]
            k = k_s[rows, cols]
            v = v_s[rows, cols].astype(BF16)
            bc = b_s[rows, cols]
            b_last = bc[HG_CHUNK - 1:HG_CHUNK, :]
            q_inter = (q * jnp.exp(bc)).astype(BF16)
            k_state = (k * jnp.exp(b_last - bc)).astype(BF16)
            decay = jnp.exp(b_last)
            mids = [bc[j * HG_SUB + HG_MID:j * HG_SUB + HG_MID + 1, :] for j in range(n_sub)]
            mid_row = jnp.concatenate([jnp.broadcast_to(m, (HG_SUB, dh)) for m in mids], axis=0)
            q_r = q * jnp.exp(bc - mid_row)
            k_r = k * jnp.exp(mid_row - bc)
            lhs, rhs = [], []
            for j in range(n_sub):
                cross = jnp.exp(jnp.minimum(mid_row - mids[j], 0.0))
                lhs.append(jnp.where(sub_of_row >= j, q_r * cross, 0.0).astype(BF16))
                rhs.append(jnp.where(sub_of_row == j, k_r, 0.0).astype(BF16))
            a = _dot_nt(jnp.concatenate(lhs, axis=1), jnp.concatenate(rhs, axis=1))
            a = jnp.where(causal, a, 0.0).astype(BF16)
            st = st_s[hd]
            o = _dot_nt(q_inter, st.astype(BF16)) + _dot(a, v)
            st_s[hd] = st * decay + _dot_tn(v, k_state)
            ms = jnp.mean(o * o, axis=-1, keepdims=True)
            y = o * lax.rsqrt(ms + NORM_EPS) * gn * gate_s[rows, cols]
            y_s[rows, cols] = y.astype(BF16)
        return carry

    lax.fori_loop(0, ts // HG_CHUNK, chunk_body, 0)
    m = _dot(y_s[...], wo_ref[...])
    o_ref[0] = x + _rmsnorm(m, gains_ref[3:4, :])


def _hgrn_layer(h, gains, w_in, gnorm, w_out, lb_raw, layer):
    bsz, seq, d = h.shape
    ts = MIX_ROWS
    dh = d // HG_HEADS
    big = pltpu.VMEM((ts, d), F32)
    return pl.pallas_call(
        functools.partial(_hgrn_kernel, layer=layer),
        grid=(bsz, seq // ts),
        in_specs=[pl.BlockSpec((1, ts, d), lambda b, s: (b, s, 0)),
                  _resident(gains.shape), _resident(w_in.shape), _resident(gnorm.shape),
                  _resident(w_out.shape), _resident(lb_raw.shape)],
        out_specs=pl.BlockSpec((1, ts, d), lambda b, s: (b, s, 0)),
        out_shape=jax.ShapeDtypeStruct(h.shape, F32),
        scratch_shapes=[big, big, big, big, big, pltpu.VMEM((ts, d), BF16),
                        pltpu.VMEM((HG_HEADS, dh, dh), F32)],
        compiler_params=_params(2),
        name="hgrn",
    )(h, gains, w_in, gnorm, w_out, lb_raw)


def _qkv_kernel(h_ref, gains_ref, wi_ref, q_ref, k_ref, v_ref):
    d = h_ref.shape[1]
    hn = _rmsnorm(h_ref[...], gains_ref[2:3, :]).astype(BF16)
    q_ref[...] = (_dot(hn, wi_ref[:, 0:d]) * DA_HEAD_DIM ** -0.5).astype(BF16)
    k_ref[...] = _dot(hn, wi_ref[:, d:2 * d]).astype(BF16)
    v_ref[...] = _dot(hn, wi_ref[:, 2 * d:3 * d]).astype(BF16)


def _qkv(h2, gains, w_in):
    n, d = h2.shape
    tile = pl.BlockSpec((FFN_ROWS, d), lambda i: (i, 0))
    out = jax.ShapeDtypeStruct((n, d), BF16)
    return pl.pallas_call(
        _qkv_kernel,
        grid=(n // FFN_ROWS,),
        in_specs=[tile, _resident(gains.shape), _resident(w_in.shape)],
        out_specs=[tile, tile, tile],
        out_shape=[out, out, out],
        compiler_params=_params(1),
        name="attn_qkv",
    )(h2, gains, w_in)


def _flash_kernel(q_ref, k_ref, v_ref, lam_ref, sub_ref, y_ref, *, lam_init):
    t = ATT_TILE
    dh2 = q_ref.shape[2]
    head = pl.program_id(1)
    qi = pl.program_id(2)
    slope = jnp.exp2(jnp.full((1, 1), -8.0 / DA_HEADS, F32) * (head + 1).astype(F32))

    q = q_ref[0]
    lane = lax.broadcasted_iota(jnp.int32, (t, dh2), 1)
    q_halves = (jnp.where(lane < DA_HEAD_DIM, q, jnp.zeros_like(q)),
                jnp.where(lane >= DA_HEAD_DIM, q, jnp.zeros_like(q)))
    kcol = lax.broadcasted_iota(jnp.int32, (1, t), 1).astype(F32)

    def update(stats, s, vt):
        m, l, acc = stats
        m_new = jnp.maximum(m, jnp.max(s, axis=-1, keepdims=True))
        alpha = jnp.exp(m - m_new)
        p = jnp.exp(s - m_new)
        l_new = alpha * l + jnp.sum(p, axis=-1, keepdims=True)
        acc_new = alpha * acc + _dot(p.astype(BF16), vt)
        return m_new, l_new, acc_new

    def tiles(j):
        k0 = pl.multiple_of(j * t, t)
        return k_ref[0, pl.ds(k0, t), :], v_ref[0, pl.ds(k0, t), :]

    def past_tile(j, carry):
        kt, vt = tiles(j)
        bias = slope * (kcol + ((j - qi) * t).astype(F32))
        return tuple(update(carry[g], _dot_nt(q_halves[g], kt) + bias, vt) for g in range(2))

    init = (jnp.full((t, 1), NEG_BIG, F32), jnp.zeros((t, 1), F32), jnp.zeros((t, dh2), F32))
    carry = lax.fori_loop(0, qi, past_tile, (init, init))

    kt, vt = tiles(qi)
    r = lax.broadcasted_iota(jnp.int32, (t, t), 0)
    c = lax.broadcasted_iota(jnp.int32, (t, t), 1)
    bias = slope * (r - jnp.abs(r - c)).astype(F32)
    chunk_bits = int(math.log2(CHUNK))
    allowed = (c >> chunk_bits) <= (r >> chunk_bits)
    stats = []
    for g in range(2):
        s = jnp.where(allowed, _dot_nt(q_halves[g], kt) + bias, NEG_BIG)
        stats.append(update(carry[g], s, vt))

    lam = lam_ref[...]
    lam_full = (jnp.exp(jnp.sum(lam[0:1] * lam[1:2], axis=-1, keepdims=True))
                - jnp.exp(jnp.sum(lam[2:3] * lam[3:4], axis=-1, keepdims=True)) + lam_init)
    o = stats[0][2] / stats[0][1] - lam_full * (stats[1][2] / stats[1][1])
    y_ref[0] = (_rmsnorm(o, sub_ref[...]) * (1.0 - lam_init)).astype(BF16)


def _flash(q, k, v, lam, subln, lam_init):
    bsz, seq, d = q.shape
    dh2 = 2 * DA_HEAD_DIM
    t = ATT_TILE
    q_spec = pl.BlockSpec((1, t, dh2), lambda b, h, i: (b, i, h))
    kv_spec = pl.BlockSpec((1, seq, dh2), lambda b, h, i: (b, 0, h))
    return pl.pallas_call(
        functools.partial(_flash_kernel, lam_init=lam_init),
        grid=(bsz, DA_HEADS, seq // t),
        in_specs=[q_spec, kv_spec, kv_spec, _resident(lam.shape), _resident(subln.shape)],
        out_specs=q_spec,
        out_shape=jax.ShapeDtypeStruct((bsz, seq, d), BF16),
        compiler_params=_params(3),
        name="attn_flash",
    )(q, k, v, lam, subln)


def _proj_residual_kernel(h_ref, y_ref, gains_ref, wo_ref, o_ref):
    m = _dot(y_ref[...], wo_ref[...])
    o_ref[...] = h_ref[...] + _rmsnorm(m, gains_ref[3:4, :])


def _proj_residual(h2, y2, gains, w_out):
    n, d = h2.shape
    tile = pl.BlockSpec((FFN_ROWS, d), lambda i: (i, 0))
    return pl.pallas_call(
        _proj_residual_kernel,
        grid=(n // FFN_ROWS,),
        in_specs=[tile, tile, _resident(gains.shape), _resident(w_out.shape)],
        out_specs=tile,
        out_shape=jax.ShapeDtypeStruct((n, d), F32),
        compiler_params=_params(1),
        name="attn_out",
    )(h2, y2, gains, w_out)


def _attn_layer(h, gains, w_in, lam, subln, w_out, layer):
    bsz, seq, d = h.shape
    lam_init = 0.8 - 0.6 * math.exp(-0.3 * layer)
    h2 = h.reshape(bsz * seq, d)
    q, k, v = _qkv(h2, gains, w_in)
    shape3 = (bsz, seq, d)
    y = _flash(q.reshape(shape3), k.reshape(shape3), v.reshape(shape3), lam, subln, lam_init)
    return _proj_residual(h2, y.reshape(bsz * seq, d), gains, w_out).reshape(shape3)


def _pool_kernel(h_ref, gains_ref, wi_ref, wg_ref, sc_ref, wo_ref, o_ref, carry_s, y_s):
    ts, d = h_ref.shape[1], h_ref.shape[2]
    gd = d // len(POOL_WINDOWS)
    halo = POOL_WINDOWS[-1]
    s_idx = pl.program_id(1)

    @pl.when(s_idx == 0)
    def _():
        carry_s[...] = jnp.zeros_like(carry_s)

    x = h_ref[0]
    hn = _rmsnorm(x, gains_ref[2:3, :]).astype(BF16)
    u = _dot(hn, wi_ref[...])
    cur = jnp.concatenate([carry_s[...], u], axis=0)
    carry_s[...] = u[ts - halo:, :]
    pos1 = s_idx * ts + lax.broadcasted_iota(jnp.int32, (ts, gd), 0) + 1
    for g, w in enumerate(POOL_WINDOWS):
        cur = cur + pltpu.roll(cur, w // 2, axis=0)
        count = jnp.minimum(pos1, w).astype(F32)
        pooled = cur[halo:, 0:gd] / count - u[:, g * gd:(g + 1) * gd]
        yg = _dot(pooled.astype(BF16), wg_ref[g]) * sc_ref[:, g * gd:(g + 1) * gd]
        y_s[:, g * gd:(g + 1) * gd] = yg.astype(BF16)
        cur = cur[:, gd:]
    m = _dot(y_s[...], wo_ref[...])
    o_ref[0] = x + _rmsnorm(m, gains_ref[3:4, :])


def _pool_layer(h, gains, w_in, w_group, scale, w_out):
    bsz, seq, d = h.shape
    ts = MIX_ROWS
    return pl.pallas_call(
        _pool_kernel,
        grid=(bsz, seq // ts),
        in_specs=[pl.BlockSpec((1, ts, d), lambda b, s: (b, s, 0)),
                  _resident(gains.shape), _resident(w_in.shape), _resident(w_group.shape),
                  _resident(scale.shape), _resident(w_out.shape)],
        out_specs=pl.BlockSpec((1, ts, d), lambda b, s: (b, s, 0)),
        out_shape=jax.ShapeDtypeStruct(h.shape, F32),
        scratch_shapes=[pltpu.VMEM((POOL_WINDOWS[-1], d), F32), pltpu.VMEM((ts, d), BF16)],
        compiler_params=_params(2),
        name="pool",
    )(h, gains, w_in, w_group, scale, w_out)


def kernel(x, norm_gains, ffn1_w_in, ffn1_w_out, ffn2_w_in, ffn2_w_out, hgrn_w_in, hgrn_gnorm,
           hgrn_w_out, hgrn_lb_raw, diff_w_in, diff_lambda, diff_subln, diff_w_out, pool_w_in,
           pool_w_group, pool_scale, pool_w_out):
    bsz, seq, d = x.shape
    depth = norm_gains.shape[0]
    n_mixers = 3
    bf = lambda w: w.astype(BF16)
    h = x
    for i in range(depth):
        kind, j = i % n_mixers, i // n_mixers
        gains = norm_gains[i]
        h = _ffn(h.reshape(bsz * seq, d), gains, bf(ffn1_w_in[i]), bf(ffn1_w_out[i]), 0, 1)
        h = h.reshape(bsz, seq, d)
        if kind == 0:
            h = _hgrn_layer(h, gains, bf(hgrn_w_in[j]), hgrn_gnorm[j][None, :], bf(hgrn_w_out[j]),
                            hgrn_lb_raw, i)
        elif kind == 1:
            h = _attn_layer(h, gains, bf(diff_w_in[j]), diff_lambda[j], diff_subln[j][None, :],
                            bf(diff_w_out[j]), i)
        else:
            h = _pool_layer(h, gains, bf(pool_w_in[j]), bf(pool_w_group[j]), pool_scale[j][None, :],
                            bf(pool_w_out[j]))
        h = _ffn(h.reshape(bsz * seq, d), gains, bf(ffn2_w_in[i]), bf(ffn2_w_out[i]), 4, 5)
        h = h.reshape(bsz, seq, d)
    return h
```

```python
import functools
import math

import jax
import jax.numpy as jnp
from jax import lax
from jax.experimental import pallas as pl
from jax.experimental.pallas import tpu as pltpu

F32 = jnp.float32
BF16 = jnp.bfloat16

NORM_EPS = 1e-6
CHUNK = 64
HG_HEADS = 8
HG_SUB = 16
HG_MID = HG_SUB // 2
DA_HEADS = 8
DA_HEAD_DIM = 64
POOL_WINDOWS = (2, 4, 8, 16)

LANES = 128
BF16_ROWS = 16
V7X_VMEM_BYTES = 64 * 2**20
VMEM_LIMIT = V7X_VMEM_BYTES * 3 // 4

FFN_ROWS = 512
FFN_COLS = 256
MIX_ROWS = 512
HG_CHUNK = 64
ATT_TILE = 512
NEG_BIG = -1e30
LOG2E = math.log2(math.e)


def _rmsnorm(x, w):
    ms = jnp.mean(x * x, axis=-1, keepdims=True)
    return x * lax.rsqrt(ms + NORM_EPS) * w


def _dot(a, b):
    return jnp.dot(a, b, preferred_element_type=F32)


def _dot_nt(a, b):
    return lax.dot_general(a, b, (((1,), (1,)), ((), ())), preferred_element_type=F32)


def _dot_tn(a, b):
    return lax.dot_general(a, b, (((0,), (0,)), ((), ())), preferred_element_type=F32)


def _silu(x):
    return x * jax.nn.sigmoid(x)


def _resident(shape):
    zeros = (0,) * len(shape)
    return pl.BlockSpec(shape, lambda *_: zeros, pipeline_mode=pl.Buffered(1))


def _params(n_grid):
    return pltpu.CompilerParams(
        dimension_semantics=("arbitrary",) * n_grid, vmem_limit_bytes=VMEM_LIMIT)


def _ffn_kernel(h_ref, gains_ref, wi_ref, wo_ref, o_ref, act_ref, *, d_ff, pre, post):
    x = h_ref[...]
    xn = _rmsnorm(x, gains_ref[pre:pre + 1, :]).astype(BF16)
    for lo in range(0, d_ff, FFN_COLS):
        g = _dot(xn, wi_ref[:, lo:lo + FFN_COLS])
        u = _dot(xn, wi_ref[:, d_ff + lo:d_ff + lo + FFN_COLS])
        act_ref[:, lo:lo + FFN_COLS] = (_silu(g) * u).astype(BF16)
    y = _dot(act_ref[...], wo_ref[...])
    o_ref[...] = x + 0.5 * _rmsnorm(y, gains_ref[post:post + 1, :])


def _ffn(h2, gains, w_in, w_out, pre, post):
    n, d = h2.shape
    d_ff = w_out.shape[0]
    return pl.pallas_call(
        functools.partial(_ffn_kernel, d_ff=d_ff, pre=pre, post=post),
        grid=(n // FFN_ROWS,),
        in_specs=[pl.BlockSpec((FFN_ROWS, d), lambda i: (i, 0)),
                  _resident(gains.shape), _resident(w_in.shape), _resident(w_out.shape)],
        out_specs=pl.BlockSpec((FFN_ROWS, d), lambda i: (i, 0)),
        out_shape=jax.ShapeDtypeStruct((n, d), F32),
        scratch_shapes=[pltpu.VMEM((FFN_ROWS, d_ff), BF16)],
        compiler_params=_params(1),
        name="ffn",
    )(h2, gains, w_in, w_out)


def _hgrn_kernel(h_ref, gains_ref, wi_ref, gn_ref, wo_ref, lb_ref, o_ref,
                 q_s, k_s, v_s, b_s, gate_s, y_s, st_s, *, layer):
    ts, d = h_ref.shape[1], h_ref.shape[2]
    dh = d // HG_HEADS
    n_sub = HG_CHUNK // HG_SUB

    @pl.when(pl.program_id(1) == 0)
    def _():
        st_s[...] = jnp.zeros_like(st_s)

    x = h_ref[0]
    hn = _rmsnorm(x, gains_ref[2:3, :]).astype(BF16)

    raw = lb_ref[...]
    e = jnp.exp(raw - jnp.max(raw, axis=0, keepdims=True))
    p = e / jnp.sum(e, axis=0, keepdims=True)
    lb = jnp.sum(p[0:layer + 1, :], axis=0, keepdims=True) - p[0:1, :]

    q_s[...] = _dot(hn, wi_ref[:, 0:d])
    f = lb + (1.0 - lb) * jax.nn.sigmoid(_dot(hn, wi_ref[:, d:2 * d]))
    k_s[...] = 1.0 - f
    b = jnp.log(f)
    row_in_chunk = lax.broadcasted_iota(jnp.int32, (ts, d), 0) & (HG_CHUNK - 1)
    shift = 1
    while shift < HG_CHUNK:
        b = b + jnp.where(row_in_chunk >= shift, pltpu.roll(b, shift, axis=0), 0.0)
        shift *= 2
    b_s[...] = b
    v_s[...] = _dot(hn, wi_ref[:, 2 * d:3 * d])
    gate_s[...] = _silu(_dot(hn, wi_ref[:, 3 * d:4 * d]))

    tri_r = lax.broadcasted_iota(jnp.int32, (HG_CHUNK, HG_CHUNK), 0)
    tri_c = lax.broadcasted_iota(jnp.int32, (HG_CHUNK, HG_CHUNK), 1)
    causal = tri_c <= tri_r
    gn = gn_ref[...]
    zero_blk = jnp.zeros((HG_SUB, dh), BF16)

    def sub(x, i):
        return x[i * HG_SUB:(i + 1) * HG_SUB, :]

    def prepare(rows, hd):
        cols = slice(hd * dh, (hd + 1) * dh)
        q = q_s[rows, cols]
        k = k_s[rows, cols]
        bc = b_s[rows, cols]
        b_last = bc[HG_CHUNK - 1:HG_CHUNK, :]
        q_inter = (q * jnp.exp(bc)).astype(BF16)
        k_state = (k * jnp.exp(b_last - bc)).astype(BF16)
        mids = [bc[j * HG_SUB + HG_MID:j * HG_SUB + HG_MID + 1, :] for j in range(n_sub)]
        mid_row = jnp.concatenate([jnp.broadcast_to(m, (HG_SUB, dh)) for m in mids], axis=0)
        q_r = q * jnp.exp(bc - mid_row)
        k_r = (k * jnp.exp(mid_row - bc)).astype(BF16)
        q_rb = q_r.astype(BF16)
        lhs, rhs = [], []
        for j in range(n_sub):
            lhs_blocks, rhs_blocks = [], []
            for i in range(n_sub):
                if i < j:
                    lhs_blocks.append(zero_blk)
                elif i == j:
                    lhs_blocks.append(sub(q_rb, i))
                else:
                    lhs_blocks.append((sub(q_r, i) * jnp.exp(mids[i] - mids[j])).astype(BF16))
                rhs_blocks.append(sub(k_r, i) if i == j else zero_blk)
            lhs.append(jnp.concatenate(lhs_blocks, axis=0))
            rhs.append(jnp.concatenate(rhs_blocks, axis=0))
        return dict(cols=cols, q_inter=q_inter, k_state=k_state, decay=jnp.exp(b_last),
                    lhs=jnp.concatenate(lhs, axis=1), rhs=jnp.concatenate(rhs, axis=1),
                    v=v_s[rows, cols].astype(BF16))

    def chunk_body(c, carry):
        r0 = pl.multiple_of(c * HG_CHUNK, HG_CHUNK)
        rows = pl.ds(r0, HG_CHUNK)
        ops = [prepare(rows, hd) for hd in range(HG_HEADS)]
        scores = [_dot_nt(op["lhs"], op["rhs"]) for op in ops]
        outs = []
        for hd, op in enumerate(ops):
            a = jnp.where(causal, scores[hd], 0.0).astype(BF16)
            st = st_s[hd]
            outs.append(_dot_nt(op["q_inter"], st.astype(BF16)) + _dot(a, op["v"]))
            st_s[hd] = st * op["decay"] + _dot_tn(op["v"], op["k_state"])
        for hd, op in enumerate(ops):
            o = outs[hd]
            ms = jnp.mean(o * o, axis=-1, keepdims=True)
            y = o * lax.rsqrt(ms + NORM_EPS) * gn * gate_s[rows, op["cols"]]
            y_s[rows, op["cols"]] = y.astype(BF16)
        return carry

    lax.fori_loop(0, ts // HG_CHUNK, chunk_body, 0)
    m = _dot(y_s[...], wo_ref[...])
    o_ref[0] = x + _rmsnorm(m, gains_ref[3:4, :])


def _hgrn_layer(h, gains, w_in, gnorm, w_out, lb_raw, layer):
    bsz, seq, d = h.shape
    ts = MIX_ROWS
    dh = d // HG_HEADS
    big = pltpu.VMEM((ts, d), F32)
    return pl.pallas_call(
        functools.partial(_hgrn_kernel, layer=layer),
        grid=(bsz, seq // ts),
        in_specs=[pl.BlockSpec((1, ts, d), lambda b, s: (b, s, 0)),
                  _resident(gains.shape), _resident(w_in.shape), _resident(gnorm.shape),
                  _resident(w_out.shape), _resident(lb_raw.shape)],
        out_specs=pl.BlockSpec((1, ts, d), lambda b, s: (b, s, 0)),
        out_shape=jax.ShapeDtypeStruct(h.shape, F32),
        scratch_shapes=[big, big, big, big, big, pltpu.VMEM((ts, d), BF16),
                        pltpu.VMEM((HG_HEADS, dh, dh), F32)],
        compiler_params=_params(2),
        name="hgrn",
    )(h, gains, w_in, gnorm, w_out, lb_raw)


def _qkv_kernel(h_ref, gains_ref, wi_ref, qt_ref, k_ref, vt_ref):
    d = h_ref.shape[2]
    hn = _rmsnorm(h_ref[0], gains_ref[2:3, :]).astype(BF16)
    q = _dot(hn, wi_ref[:, 0:d]) * (DA_HEAD_DIM ** -0.5 * LOG2E)
    qt_ref[0] = q.astype(BF16).T
    k_ref[0] = _dot(hn, wi_ref[:, d:2 * d]).astype(BF16)
    vt_ref[0] = _dot(hn, wi_ref[:, 2 * d:3 * d]).astype(BF16).T


def _qkv(h, gains, w_in):
    bsz, seq, d = h.shape
    ts = FFN_ROWS
    rows = pl.BlockSpec((1, ts, d), lambda b, s: (b, s, 0))
    cols = pl.BlockSpec((1, d, ts), lambda b, s: (b, 0, s))
    return pl.pallas_call(
        _qkv_kernel,
        grid=(bsz, seq // ts),
        in_specs=[rows, _resident(gains.shape), _resident(w_in.shape)],
        out_specs=[cols, rows, cols],
        out_shape=[jax.ShapeDtypeStruct((bsz, d, seq), BF16), jax.ShapeDtypeStruct((bsz, seq, d), BF16),
                   jax.ShapeDtypeStruct((bsz, d, seq), BF16)],
        compiler_params=_params(2),
        name="attn_qkv",
    )(h, gains, w_in)


def _split_bf16(x, parts):
    out = []
    for _ in range(parts):
        hi = x.astype(BF16).astype(F32)
        out.append(hi)
        x = x - hi
    return out


def _flash_kernel(qt_ref, k_ref, vt_ref, lam_ref, sub_ref, y_ref,
                  qx_ref, kf_ref, s_a, s_b, p_a, p_b, acc_ref, *, lam_init):
    t = ATT_TILE
    dh2 = qt_ref.shape[1]
    head = pl.program_id(1)
    qi = pl.program_id(2)
    slope = jnp.exp2(jnp.full((1, 1), -8.0 / DA_HEADS, F32) * (head + 1).astype(F32))
    a = slope * LOG2E

    low_bits = 4
    a_parts = _split_bf16(a, 3)
    frow = lax.broadcasted_iota(jnp.int32, (dh2, t), 0)
    feat_q = jnp.zeros((dh2, t), F32)
    for n, part in enumerate(a_parts):
        feat_q = jnp.where(frow == n, part * (1 << low_bits), feat_q)
        feat_q = jnp.where(frow == len(a_parts) + n, part, feat_q)
    feat_q = feat_q.astype(BF16)
    key_idx = lax.broadcasted_iota(jnp.int32, (t, dh2), 0)
    flane = lax.broadcasted_iota(jnp.int32, (t, dh2), 1)
    feat_k = jnp.where(flane < len(a_parts), key_idx >> low_bits,
                       jnp.where(flane < 2 * len(a_parts), key_idx & ((1 << low_bits) - 1), 0))
    feat_k = feat_k.astype(F32).astype(BF16)

    qt = qt_ref[0]
    zero = jnp.zeros_like(qt)
    qx_ref[0] = jnp.concatenate([jnp.where(frow < DA_HEAD_DIM, qt, zero), feat_q], axis=0)
    qx_ref[1] = jnp.concatenate([jnp.where(frow >= DA_HEAD_DIM, qt, zero), feat_q], axis=0)
    kf_ref[...] = feat_k
    p_b[...] = jnp.zeros_like(p_b)
    acc_ref[...] = jnp.zeros_like(acc_ref)

    def scores(i, s_ref):
        k0 = pl.multiple_of(i * t, t)
        k_ext = jnp.concatenate([k_ref[0, pl.ds(k0, t), :], kf_ref[...]], axis=1)
        tops = []
        for g in range(2):
            st = _dot(k_ext, qx_ref[g])
            s_ref[g] = st
            tops.append(jnp.max(st, axis=0, keepdims=True))
        return tuple(tops)

    def softmax(m, st, top, offset):
        m_new = jnp.maximum(m, top + offset)
        alpha = jnp.exp2(m - m_new)
        p = jnp.exp2(st - (m_new - offset))
        return m_new, alpha, p.astype(BF16)

    ones_rows = jnp.ones((acc_ref.shape[1] - dh2, t), BF16)

    def values(i, alphas, p_of_g):
        k0 = pl.multiple_of(jnp.maximum(i, 0) * t, t)
        vt = jnp.concatenate([vt_ref[0, :, pl.ds(k0, t)], ones_rows], axis=0)
        for g in range(2):
            acc_ref[g] = alphas[g] * acc_ref[g] + _dot(vt, p_of_g(g))

    def past_step(i, carry, s_cur, s_next, p_cur, p_prev):
        ms, alphas, tops = carry
        next_tops = scores(i + 1, s_next)
        values(i - 1, alphas, lambda g: p_prev[g])
        offset = a * ((i - qi) * t).astype(F32)
        new_ms, new_alphas = [], []
        for g in range(2):
            m_new, alpha, p = softmax(ms[g], s_cur[g], tops[g], offset)
            p_cur[g] = p
            new_ms.append(m_new)
            new_alphas.append(alpha)
        return tuple(new_ms), tuple(new_alphas), next_tops

    def diagonal_step(carry, s_cur, p_prev):
        ms, alphas, _ = carry
        values(qi - 1, alphas, lambda g: p_prev[g])
        key = lax.broadcasted_iota(jnp.int32, (t, t), 0)
        qry = lax.broadcasted_iota(jnp.int32, (t, t), 1)
        mirror = jnp.where(key > qry, (qry - key).astype(F32) * (2.0 * a), 0.0)
        chunk_bits = int(math.log2(CHUNK))
        allowed = (key >> chunk_bits) <= (qry >> chunk_bits)
        ps, new_alphas = [], []
        for g in range(2):
            st = jnp.where(allowed, s_cur[g] + mirror, NEG_BIG)
            _, alpha, p = softmax(ms[g], st, jnp.max(st, axis=0, keepdims=True), 0.0)
            ps.append(p)
            new_alphas.append(alpha)
        values(qi, new_alphas, lambda g: ps[g])

    def pair(ii, carry):
        carry = past_step(2 * ii, carry, s_a, s_b, p_a, p_b)
        return past_step(2 * ii + 1, carry, s_b, s_a, p_b, p_a)

    first_tops = scores(0, s_a)
    init = ((jnp.full((1, t), NEG_BIG, F32),) * 2, (jnp.ones((1, t), F32),) * 2, first_tops)
    carry = lax.fori_loop(0, qi // 2, pair, init)

    @pl.when(qi % 2 == 1)
    def _():
        diagonal_step(past_step(qi - 1, carry, s_a, s_b, p_a, p_b), s_b, p_a)

    @pl.when(qi % 2 == 0)
    def _():
        diagonal_step(carry, s_a, p_b)

    lam = lam_ref[...]
    lam_full = (jnp.exp(jnp.sum(lam[0:1] * lam[1:2], axis=-1, keepdims=True))
                - jnp.exp(jnp.sum(lam[2:3] * lam[3:4], axis=-1, keepdims=True)) + lam_init)
    ot = (acc_ref[0, 0:dh2, :] / acc_ref[0, dh2:dh2 + 1, :]
          - lam_full * (acc_ref[1, 0:dh2, :] / acc_ref[1, dh2:dh2 + 1, :]))
    ms = jnp.mean(ot * ot, axis=0, keepdims=True)
    yt = ot * lax.rsqrt(ms + NORM_EPS) * sub_ref[...] * (1.0 - lam_init)
    y_ref[0] = yt.T.astype(BF16)


def _flash(qt, k, vt, lam, subln_col, lam_init):
    bsz, seq, d = k.shape
    dh2 = 2 * DA_HEAD_DIM
    t = ATT_TILE
    s_buf = pltpu.VMEM((2, t, t), F32)
    p_buf = pltpu.VMEM((2, t, t), BF16)
    return pl.pallas_call(
        functools.partial(_flash_kernel, lam_init=lam_init),
        grid=(bsz, DA_HEADS, seq // t),
        in_specs=[pl.BlockSpec((1, dh2, t), lambda b, h, i: (b, h, i)),
                  pl.BlockSpec((1, seq, dh2), lambda b, h, i: (b, 0, h)),
                  pl.BlockSpec((1, dh2, seq), lambda b, h, i: (b, h, 0)),
                  _resident(lam.shape), _resident(subln_col.shape)],
        out_specs=pl.BlockSpec((1, t, dh2), lambda b, h, i: (b, i, h)),
        out_shape=jax.ShapeDtypeStruct((bsz, seq, d), BF16),
        scratch_shapes=[pltpu.VMEM((2, 2 * dh2, t), BF16), pltpu.VMEM((t, dh2), BF16),
                        s_buf, s_buf, p_buf, p_buf, pltpu.VMEM((2, dh2 + BF16_ROWS, t), F32)],
        compiler_params=_params(3),
        name="attn_flash",
    )(qt, k, vt, lam, subln_col)


def _proj_residual_kernel(h_ref, y_ref, gains_ref, wo_ref, o_ref):
    m = _dot(y_ref[...], wo_ref[...])
    o_ref[...] = h_ref[...] + _rmsnorm(m, gains_ref[3:4, :])


def _proj_residual(h2, y2, gains, w_out):
    n, d = h2.shape
    tile = pl.BlockSpec((FFN_ROWS, d), lambda i: (i, 0))
    return pl.pallas_call(
        _proj_residual_kernel,
        grid=(n // FFN_ROWS,),
        in_specs=[tile, tile, _resident(gains.shape), _resident(w_out.shape)],
        out_specs=tile,
        out_shape=jax.ShapeDtypeStruct((n, d), F32),
        compiler_params=_params(1),
        name="attn_out",
    )(h2, y2, gains, w_out)


def _attn_layer(h, gains, w_in, lam, subln, w_out, layer):
    bsz, seq, d = h.shape
    lam_init = 0.8 - 0.6 * math.exp(-0.3 * layer)
    qt, k, vt = _qkv(h, gains, w_in)
    y = _flash(qt, k, vt, lam, subln.reshape(-1, 1), lam_init)
    out = _proj_residual(h.reshape(bsz * seq, d), y.reshape(bsz * seq, d), gains, w_out)
    return out.reshape(bsz, seq, d)


def _pool_kernel(h_ref, gains_ref, wi_ref, wg_ref, sc_ref, wo_ref, o_ref, carry_s, y_s):
    ts, d = h_ref.shape[1], h_ref.shape[2]
    gd = d // len(POOL_WINDOWS)
    halo = POOL_WINDOWS[-1]
    s_idx = pl.program_id(1)

    @pl.when(s_idx == 0)
    def _():
        carry_s[...] = jnp.zeros_like(carry_s)

    x = h_ref[0]
    hn = _rmsnorm(x, gains_ref[2:3, :]).astype(BF16)
    u = _dot(hn, wi_ref[...])
    cur = jnp.concatenate([carry_s[...], u], axis=0)
    carry_s[...] = u[ts - halo:, :]
    pos1 = s_idx * ts + lax.broadcasted_iota(jnp.int32, (ts, gd), 0) + 1
    for g, w in enumerate(POOL_WINDOWS):
        cur = cur + pltpu.roll(cur, w // 2, axis=0)
        count = jnp.minimum(pos1, w).astype(F32)
        pooled = cur[halo:, 0:gd] / count - u[:, g * gd:(g + 1) * gd]
        yg = _dot(pooled.astype(BF16), wg_ref[g]) * sc_ref[:, g * gd:(g + 1) * gd]
        y_s[:, g * gd:(g + 1) * gd] = yg.astype(BF16)
        cur = cur[:, gd:]
    m = _dot(y_s[...], wo_ref[...])
    o_ref[0] = x + _rmsnorm(m, gains_ref[3:4, :])


def _pool_layer(h, gains, w_in, w_group, scale, w_out):
    bsz, seq, d = h.shape
    ts = MIX_ROWS
    return pl.pallas_call(
        _pool_kernel,
        grid=(bsz, seq // ts),
        in_specs=[pl.BlockSpec((1, ts, d), lambda b, s: (b, s, 0)),
                  _resident(gains.shape), _resident(w_in.shape), _resident(w_group.shape),
                  _resident(scale.shape), _resident(w_out.shape)],
        out_specs=pl.BlockSpec((1, ts, d), lambda b, s: (b, s, 0)),
        out_shape=jax.ShapeDtypeStruct(h.shape, F32),
        scratch_shapes=[pltpu.VMEM((POOL_WINDOWS[-1], d), F32), pltpu.VMEM((ts, d), BF16)],
        compiler_params=_params(2),
        name="pool",
    )(h, gains, w_in, w_group, scale, w_out)


def kernel(x, norm_gains, ffn1_w_in, ffn1_w_out, ffn2_w_in, ffn2_w_out, hgrn_w_in, hgrn_gnorm,
           hgrn_w_out, hgrn_lb_raw, diff_w_in, diff_lambda, diff_subln, diff_w_out, pool_w_in,
           pool_w_group, pool_scale, pool_w_out):
    bsz, seq, d = x.shape
    depth = norm_gains.shape[0]
    n_mixers = 3
    bf = lambda w: w.astype(BF16)
    h = x
    for i in range(depth):
        kind, j = i % n_mixers, i // n_mixers
        gains = norm_gains[i]
        h = _ffn(h.reshape(bsz * seq, d), gains, bf(ffn1_w_in[i]), bf(ffn1_w_out[i]), 0, 1)
        h = h.reshape(bsz, seq, d)
        if kind == 0:
            h = _hgrn_layer(h, gains, bf(hgrn_w_in[j]), hgrn_gnorm[j][None, :], bf(hgrn_w_out[j]),
                            hgrn_lb_raw, i)
        elif kind == 1:
            h = _attn_layer(h, gains, bf(diff_w_in[j]), diff_lambda[j], diff_subln[j][None, :],
                            bf(diff_w_out[j]), i)
        else:
            h = _pool_layer(h, gains, bf(pool_w_in[j]), bf(pool_w_group[j]), pool_scale[j][None, :],
                            bf(pool_w_out[j]))
        h = _ffn(h.reshape(bsz * seq, d), gains, bf(ffn2_w_in[i]), bf(ffn2_w_out[i]), 4, 5)
        h = h.reshape(bsz, seq, d)
    return h
```

```python
import functools
import math

import jax
import jax.numpy as jnp
from jax import lax
from jax.experimental import pallas as pl
from jax.experimental.pallas import tpu as pltpu

F32 = jnp.float32
BF16 = jnp.bfloat16

NORM_EPS = 1e-6
CHUNK = 64
HG_HEADS = 8
HG_SUB = 16
HG_MID = HG_SUB // 2
DA_HEADS = 8
DA_HEAD_DIM = 64
POOL_WINDOWS = (2, 4, 8, 16)

LANES = 128
BF16_ROWS = 16
V7X_VMEM_BYTES = 64 * 2**20
VMEM_LIMIT = V7X_VMEM_BYTES * 3 // 4

FFN_ROWS = 1024
PROJ_ROWS = 1024
PROJ_SUB_ROWS = 256
FFN_SUB_ROWS = 256
FFN_COLS = 256
MIX_ROWS = 512
HG_CHUNK = 64
HG_SUB_ROWS = 256
ATT_TILE = 512
NEG_BIG = -1e30
LOG2E = math.log2(math.e)


def _rmsnorm(x, w):
    ms = jnp.mean(x * x, axis=-1, keepdims=True)
    return x * lax.rsqrt(ms + NORM_EPS) * w


def _dot(a, b):
    return jnp.dot(a, b, preferred_element_type=F32)


def _dot_nt(a, b):
    return lax.dot_general(a, b, (((1,), (1,)), ((), ())), preferred_element_type=F32)


def _dot_tn(a, b):
    return lax.dot_general(a, b, (((0,), (0,)), ((), ())), preferred_element_type=F32)


def _silu(x):
    return x * jax.nn.sigmoid(x)


def _resident(shape):
    zeros = (0,) * len(shape)
    return pl.BlockSpec(shape, lambda *_: zeros, pipeline_mode=pl.Buffered(1))


def _params(n_grid):
    return pltpu.CompilerParams(
        dimension_semantics=("arbitrary",) * n_grid, vmem_limit_bytes=VMEM_LIMIT)


def _ffn_kernel(h_ref, gains_ref, wi_ref, wo_ref, o_ref, act_ref, *, d_ff, pre, post):
    for r0 in range(0, h_ref.shape[0], FFN_SUB_ROWS):
        rows = slice(r0, r0 + FFN_SUB_ROWS)
        x = h_ref[rows, :]
        xn = _rmsnorm(x, gains_ref[pre:pre + 1, :]).astype(BF16)
        for lo in range(0, d_ff, FFN_COLS):
            g = _dot(xn, wi_ref[:, lo:lo + FFN_COLS])
            u = _dot(xn, wi_ref[:, d_ff + lo:d_ff + lo + FFN_COLS])
            act_ref[rows, lo:lo + FFN_COLS] = (_silu(g) * u).astype(BF16)
        y = _dot(act_ref[rows, :], wo_ref[...])
        o_ref[rows, :] = x + 0.5 * _rmsnorm(y, gains_ref[post:post + 1, :])


def _ffn(h2, gains, w_in, w_out, pre, post):
    n, d = h2.shape
    d_ff = w_out.shape[0]
    return pl.pallas_call(
        functools.partial(_ffn_kernel, d_ff=d_ff, pre=pre, post=post),
        grid=(n // FFN_ROWS,),
        in_specs=[pl.BlockSpec((FFN_ROWS, d), lambda i: (i, 0)),
                  _resident(gains.shape), _resident(w_in.shape), _resident(w_out.shape)],
        out_specs=pl.BlockSpec((FFN_ROWS, d), lambda i: (i, 0)),
        out_shape=jax.ShapeDtypeStruct((n, d), F32),
        scratch_shapes=[pltpu.VMEM((FFN_ROWS, d_ff), BF16)],
        compiler_params=_params(1),
        name="ffn",
    )(h2, gains, w_in, w_out)


def _hgrn_kernel(h_ref, gains_ref, wi_ref, gn_ref, wo_ref, lb_ref, o_ref,
                 q_s, k_s, v_s, b_s, gate_s, y_s, st_s, *, layer):
    ts, d = h_ref.shape[1], h_ref.shape[2]
    dh = d // HG_HEADS
    n_sub = HG_CHUNK // HG_SUB

    @pl.when(pl.program_id(1) == 0)
    def _():
        st_s[...] = jnp.zeros_like(st_s)

    raw = lb_ref[...]
    e = jnp.exp(raw - jnp.max(raw, axis=0, keepdims=True))
    p = e / jnp.sum(e, axis=0, keepdims=True)
    lb = jnp.sum(p[0:layer + 1, :], axis=0, keepdims=True) - p[0:1, :]

    row_in_chunk = lax.broadcasted_iota(jnp.int32, (HG_SUB_ROWS, d), 0) & (HG_CHUNK - 1)
    tri_r = lax.broadcasted_iota(jnp.int32, (HG_CHUNK, HG_CHUNK), 0)
    tri_c = lax.broadcasted_iota(jnp.int32, (HG_CHUNK, HG_CHUNK), 1)
    causal = tri_c <= tri_r
    gn = gn_ref[...]
    zero_blk = jnp.zeros((HG_SUB, dh), BF16)

    def project(rows):
        x = h_ref[0, rows, :]
        hn = _rmsnorm(x, gains_ref[2:3, :]).astype(BF16)
        q_s[rows, :] = _dot(hn, wi_ref[:, 0:d])
        f = lb + (1.0 - lb) * jax.nn.sigmoid(_dot(hn, wi_ref[:, d:2 * d]))
        k_s[rows, :] = 1.0 - f
        b = jnp.log(f)
        shift = 1
        while shift < HG_CHUNK:
            b = b + jnp.where(row_in_chunk >= shift, pltpu.roll(b, shift, axis=0), 0.0)
            shift *= 2
        b_s[rows, :] = b
        v_s[rows, :] = _dot(hn, wi_ref[:, 2 * d:3 * d])
        gate_s[rows, :] = _silu(_dot(hn, wi_ref[:, 3 * d:4 * d]))
        return x

    def sub(x, i):
        return x[i * HG_SUB:(i + 1) * HG_SUB, :]

    def prepare(rows, hd):
        cols = slice(hd * dh, (hd + 1) * dh)
        q = q_s[rows, cols]
        k = k_s[rows, cols]
        bc = b_s[rows, cols]
        b_last = bc[HG_CHUNK - 1:HG_CHUNK, :]
        q_inter = (q * jnp.exp(bc)).astype(BF16)
        k_state = (k * jnp.exp(b_last - bc)).astype(BF16)
        mids = [bc[j * HG_SUB + HG_MID:j * HG_SUB + HG_MID + 1, :] for j in range(n_sub)]
        mid_row = jnp.concatenate([jnp.broadcast_to(m, (HG_SUB, dh)) for m in mids], axis=0)
        q_r = q * jnp.exp(bc - mid_row)
        k_r = (k * jnp.exp(mid_row - bc)).astype(BF16)
        q_rb = q_r.astype(BF16)
        lhs, rhs = [], []
        for j in range(n_sub):
            lhs_blocks, rhs_blocks = [], []
            for i in range(n_sub):
                if i < j:
                    lhs_blocks.append(zero_blk)
                elif i == j:
                    lhs_blocks.append(sub(q_rb, i))
                else:
                    lhs_blocks.append((sub(q_r, i) * jnp.exp(mids[i] - mids[j])).astype(BF16))
                rhs_blocks.append(sub(k_r, i) if i == j else zero_blk)
            lhs.append(jnp.concatenate(lhs_blocks, axis=0))
            rhs.append(jnp.concatenate(rhs_blocks, axis=0))
        return dict(cols=cols, q_inter=q_inter, k_state=k_state, decay=jnp.exp(b_last),
                    lhs=jnp.concatenate(lhs, axis=1), rhs=jnp.concatenate(rhs, axis=1),
                    v=v_s[rows, cols].astype(BF16))

    def mix_chunk(rows):
        ops = [prepare(rows, hd) for hd in range(HG_HEADS)]
        scores = [_dot_nt(op["lhs"], op["rhs"]) for op in ops]
        outs = []
        for hd, op in enumerate(ops):
            a = jnp.where(causal, scores[hd], 0.0).astype(BF16)
            st = st_s[hd]
            outs.append(_dot_nt(op["q_inter"], st.astype(BF16)) + _dot(a, op["v"]))
            st_s[hd] = st * op["decay"] + _dot_tn(op["v"], op["k_state"])
        for hd, op in enumerate(ops):
            o = outs[hd]
            ms = jnp.mean(o * o, axis=-1, keepdims=True)
            y = o * lax.rsqrt(ms + NORM_EPS) * gn * gate_s[rows, op["cols"]]
            y_s[rows, op["cols"]] = y.astype(BF16)

    for r0 in range(0, ts, HG_SUB_ROWS):
        rows = slice(r0, r0 + HG_SUB_ROWS)
        x = project(rows)
        for c0 in range(r0, r0 + HG_SUB_ROWS, HG_CHUNK):
            mix_chunk(slice(c0, c0 + HG_CHUNK))
        m = _dot(y_s[rows, :], wo_ref[...])
        o_ref[0, rows, :] = x + _rmsnorm(m, gains_ref[3:4, :])


def _hgrn_layer(h, gains, w_in, gnorm, w_out, lb_raw, layer):
    bsz, seq, d = h.shape
    ts = MIX_ROWS
    dh = d // HG_HEADS
    big = pltpu.VMEM((ts, d), F32)
    return pl.pallas_call(
        functools.partial(_hgrn_kernel, layer=layer),
        grid=(bsz, seq // ts),
        in_specs=[pl.BlockSpec((1, ts, d), lambda b, s: (b, s, 0)),
                  _resident(gains.shape), _resident(w_in.shape), _resident(gnorm.shape),
                  _resident(w_out.shape), _resident(lb_raw.shape)],
        out_specs=pl.BlockSpec((1, ts, d), lambda b, s: (b, s, 0)),
        out_shape=jax.ShapeDtypeStruct(h.shape, F32),
        scratch_shapes=[big, big, big, big, big, pltpu.VMEM((ts, d), BF16),
                        pltpu.VMEM((HG_HEADS, dh, dh), F32)],
        compiler_params=_params(2),
        name="hgrn",
    )(h, gains, w_in, gnorm, w_out, lb_raw)


def _qkv_kernel(h_ref, gains_ref, wi_ref, qt_ref, k_ref, vt_ref):
    d = h_ref.shape[2]
    for r0 in range(0, h_ref.shape[1], PROJ_SUB_ROWS):
        rows = slice(r0, r0 + PROJ_SUB_ROWS)
        hn = _rmsnorm(h_ref[0, rows, :], gains_ref[2:3, :]).astype(BF16)
        q = _dot(hn, wi_ref[:, 0:d]) * (DA_HEAD_DIM ** -0.5 * LOG2E)
        qt_ref[0, :, rows] = q.astype(BF16).T
        k_ref[0, rows, :] = _dot(hn, wi_ref[:, d:2 * d]).astype(BF16)
        vt_ref[0, :, rows] = _dot(hn, wi_ref[:, 2 * d:3 * d]).astype(BF16).T


def _qkv(h, gains, w_in):
    bsz, seq, d = h.shape
    ts = PROJ_ROWS
    rows = pl.BlockSpec((1, ts, d), lambda b, s: (b, s, 0))
    cols = pl.BlockSpec((1, d, ts), lambda b, s: (b, 0, s))
    return pl.pallas_call(
        _qkv_kernel,
        grid=(bsz, seq // ts),
        in_specs=[rows, _resident(gains.shape), _resident(w_in.shape)],
        out_specs=[cols, rows, cols],
        out_shape=[jax.ShapeDtypeStruct((bsz, d, seq), BF16), jax.ShapeDtypeStruct((bsz, seq, d), BF16),
                   jax.ShapeDtypeStruct((bsz, d, seq), BF16)],
        compiler_params=_params(2),
        name="attn_qkv",
    )(h, gains, w_in)


def _split_bf16(x, parts):
    out = []
    for _ in range(parts):
        hi = x.astype(BF16).astype(F32)
        out.append(hi)
        x = x - hi
    return out


def _flash_kernel(qt_ref, k_ref, vt_ref, lam_ref, sub_ref, y_ref,
                  qx_ref, kf_ref, s_a, s_b, p_a, p_b, acc_ref, diag_ref, *, lam_init):
    t = ATT_TILE
    dh2 = qt_ref.shape[1]
    head = pl.program_id(1)
    qi = pl.program_id(2)
    slope = jnp.exp2(jnp.full((1, 1), -8.0 / DA_HEADS, F32) * (head + 1).astype(F32))
    a = slope * LOG2E

    low_bits = 4
    n_parts = 3
    frow = lax.broadcasted_iota(jnp.int32, (dh2, t), 0)

    @pl.when(qi == 0)
    def _():
        feat_q = jnp.zeros((dh2, t), F32)
        for n, part in enumerate(_split_bf16(a, n_parts)):
            feat_q = jnp.where(frow == n, part * (1 << low_bits), feat_q)
            feat_q = jnp.where(frow == n_parts + n, part, feat_q)
        qx_ref[0, dh2:, :] = feat_q.astype(BF16)
        qx_ref[1, dh2:, :] = feat_q.astype(BF16)
        key_idx = lax.broadcasted_iota(jnp.int32, (t, dh2), 0)
        flane = lax.broadcasted_iota(jnp.int32, (t, dh2), 1)
        feat_k = jnp.where(flane < n_parts, key_idx >> low_bits,
                           jnp.where(flane < 2 * n_parts, key_idx & ((1 << low_bits) - 1), 0))
        kf_ref[...] = feat_k.astype(F32).astype(BF16)
        key = lax.broadcasted_iota(jnp.int32, (t, t), 0)
        qry = lax.broadcasted_iota(jnp.int32, (t, t), 1)
        mirror = jnp.minimum(qry - key, 0).astype(F32) * (2.0 * a)
        chunk_bits = int(math.log2(CHUNK))
        diag_ref[...] = jnp.where((key >> chunk_bits) <= (qry >> chunk_bits), mirror, NEG_BIG)

    qt = qt_ref[0]
    zero = jnp.zeros_like(qt)
    qx_ref[0, :dh2, :] = jnp.where(frow < DA_HEAD_DIM, qt, zero)
    qx_ref[1, :dh2, :] = jnp.where(frow >= DA_HEAD_DIM, qt, zero)
    p_b[...] = jnp.zeros_like(p_b)
    acc_ref[...] = jnp.zeros_like(acc_ref)

    def scores(i, s_ref):
        k0 = pl.multiple_of(i * t, t)
        k_ext = jnp.concatenate([k_ref[0, pl.ds(k0, t), :], kf_ref[...]], axis=1)
        tops = []
        for g in range(2):
            st = _dot(k_ext, qx_ref[g])
            s_ref[g] = st
            tops.append(jnp.max(st, axis=0, keepdims=True))
        return tuple(tops)

    def softmax(m, st, top, offset):
        m_new = jnp.maximum(m, top + offset)
        alpha = jnp.exp2(m - m_new)
        p = jnp.exp2(st - (m_new - offset))
        return m_new, alpha, p.astype(BF16)

    ones_rows = jnp.ones((acc_ref.shape[1] - dh2, t), BF16)

    def values(i, alphas, p_of_g):
        k0 = pl.multiple_of(jnp.maximum(i, 0) * t, t)
        vt = jnp.concatenate([vt_ref[0, :, pl.ds(k0, t)], ones_rows], axis=0)
        for g in range(2):
            acc_ref[g] = alphas[g] * acc_ref[g] + _dot(vt, p_of_g(g))

    def past_step(i, carry, s_cur, s_next, p_cur, p_prev):
        ms, alphas, tops = carry
        next_tops = scores(i + 1, s_next)
        values(i - 1, alphas, lambda g: p_prev[g])
        offset = a * ((i - qi) * t).astype(F32)
        new_ms, new_alphas = [], []
        for g in range(2):
            m_new, alpha, p = softmax(ms[g], s_cur[g], tops[g], offset)
            p_cur[g] = p
            new_ms.append(m_new)
            new_alphas.append(alpha)
        return tuple(new_ms), tuple(new_alphas), next_tops

    def diagonal_step(carry, s_cur, p_prev):
        ms, alphas, _ = carry
        values(qi - 1, alphas, lambda g: p_prev[g])
        ps, new_alphas = [], []
        for g in range(2):
            st = s_cur[g] + diag_ref[...]
            _, alpha, p = softmax(ms[g], st, jnp.max(st, axis=0, keepdims=True), 0.0)
            ps.append(p)
            new_alphas.append(alpha)
        values(qi, new_alphas, lambda g: ps[g])

    def pair(ii, carry):
        carry = past_step(2 * ii, carry, s_a, s_b, p_a, p_b)
        return past_step(2 * ii + 1, carry, s_b, s_a, p_b, p_a)

    first_tops = scores(0, s_a)
    init = ((jnp.full((1, t), NEG_BIG, F32),) * 2, (jnp.ones((1, t), F32),) * 2, first_tops)
    carry = lax.fori_loop(0, qi // 2, pair, init)

    @pl.when(qi % 2 == 1)
    def _():
        diagonal_step(past_step(qi - 1, carry, s_a, s_b, p_a, p_b), s_b, p_a)

    @pl.when(qi % 2 == 0)
    def _():
        diagonal_step(carry, s_a, p_b)

    lam = lam_ref[...]
    lam_full = (jnp.exp(jnp.sum(lam[0:1] * lam[1:2], axis=-1, keepdims=True))
                - jnp.exp(jnp.sum(lam[2:3] * lam[3:4], axis=-1, keepdims=True)) + lam_init)
    ot = (acc_ref[0, 0:dh2, :] / acc_ref[0, dh2:dh2 + 1, :]
          - lam_full * (acc_ref[1, 0:dh2, :] / acc_ref[1, dh2:dh2 + 1, :]))
    ms = jnp.mean(ot * ot, axis=0, keepdims=True)
    yt = ot * lax.rsqrt(ms + NORM_EPS) * sub_ref[...] * (1.0 - lam_init)
    y_ref[0] = yt.T.astype(BF16)


def _flash(qt, k, vt, lam, subln_col, lam_init):
    bsz, seq, d = k.shape
    dh2 = 2 * DA_HEAD_DIM
    t = ATT_TILE
    s_buf = pltpu.VMEM((2, t, t), F32)
    p_buf = pltpu.VMEM((2, t, t), BF16)
    return pl.pallas_call(
        functools.partial(_flash_kernel, lam_init=lam_init),
        grid=(bsz, DA_HEADS, seq // t),
        in_specs=[pl.BlockSpec((1, dh2, t), lambda b, h, i: (b, h, i)),
                  pl.BlockSpec((1, seq, dh2), lambda b, h, i: (b, 0, h)),
                  pl.BlockSpec((1, dh2, seq), lambda b, h, i: (b, h, 0)),
                  _resident(lam.shape), _resident(subln_col.shape)],
        out_specs=pl.BlockSpec((1, t, dh2), lambda b, h, i: (b, i, h)),
        out_shape=jax.ShapeDtypeStruct((bsz, seq, d), BF16),
        scratch_shapes=[pltpu.VMEM((2, 2 * dh2, t), BF16), pltpu.VMEM((t, dh2), BF16),
                        s_buf, s_buf, p_buf, p_buf, pltpu.VMEM((2, dh2 + BF16_ROWS, t), F32),
                        pltpu.VMEM((t, t), F32)],
        compiler_params=_params(3),
        name="attn_flash",
    )(qt, k, vt, lam, subln_col)


def _proj_residual_kernel(h_ref, y_ref, gains_ref, wo_ref, o_ref):
    for r0 in range(0, h_ref.shape[0], PROJ_SUB_ROWS):
        rows = slice(r0, r0 + PROJ_SUB_ROWS)
        m = _dot(y_ref[rows, :], wo_ref[...])
        o_ref[rows, :] = h_ref[rows, :] + _rmsnorm(m, gains_ref[3:4, :])


def _proj_residual(h2, y2, gains, w_out):
    n, d = h2.shape
    tile = pl.BlockSpec((PROJ_ROWS, d), lambda i: (i, 0))
    return pl.pallas_call(
        _proj_residual_kernel,
        grid=(n // PROJ_ROWS,),
        in_specs=[tile, tile, _resident(gains.shape), _resident(w_out.shape)],
        out_specs=tile,
        out_shape=jax.ShapeDtypeStruct((n, d), F32),
        compiler_params=_params(1),
        name="attn_out",
    )(h2, y2, gains, w_out)


def _attn_layer(h, gains, w_in, lam, subln, w_out, layer):
    bsz, seq, d = h.shape
    lam_init = 0.8 - 0.6 * math.exp(-0.3 * layer)
    qt, k, vt = _qkv(h, gains, w_in)
    y = _flash(qt, k, vt, lam, subln.reshape(-1, 1), lam_init)
    out = _proj_residual(h.reshape(bsz * seq, d), y.reshape(bsz * seq, d), gains, w_out)
    return out.reshape(bsz, seq, d)


def _pool_kernel(h_ref, gains_ref, wi_ref, wg_ref, sc_ref, wo_ref, o_ref, carry_s, y_s):
    ts, d = h_ref.shape[1], h_ref.shape[2]
    gd = d // len(POOL_WINDOWS)
    halo = POOL_WINDOWS[-1]
    s_idx = pl.program_id(1)

    @pl.when(s_idx == 0)
    def _():
        carry_s[...] = jnp.zeros_like(carry_s)

    prev_tail = carry_s[...]
    for r0 in range(0, ts, PROJ_SUB_ROWS):
        rows = slice(r0, r0 + PROJ_SUB_ROWS)
        x = h_ref[0, rows, :]
        hn = _rmsnorm(x, gains_ref[2:3, :]).astype(BF16)
        u = _dot(hn, wi_ref[...])
        cur = jnp.concatenate([prev_tail, u], axis=0)
        prev_tail = u[PROJ_SUB_ROWS - halo:, :]
        pos1 = s_idx * ts + r0 + lax.broadcasted_iota(jnp.int32, (PROJ_SUB_ROWS, gd), 0) + 1
        for g, w in enumerate(POOL_WINDOWS):
            cur = cur + pltpu.roll(cur, w // 2, axis=0)
            count = jnp.minimum(pos1, w).astype(F32)
            pooled = cur[halo:, 0:gd] / count - u[:, g * gd:(g + 1) * gd]
            yg = _dot(pooled.astype(BF16), wg_ref[g]) * sc_ref[:, g * gd:(g + 1) * gd]
            y_s[rows, g * gd:(g + 1) * gd] = yg.astype(BF16)
            cur = cur[:, gd:]
        m = _dot(y_s[rows, :], wo_ref[...])
        o_ref[0, rows, :] = x + _rmsnorm(m, gains_ref[3:4, :])
    carry_s[...] = prev_tail


def _pool_layer(h, gains, w_in, w_group, scale, w_out):
    bsz, seq, d = h.shape
    ts = MIX_ROWS
    return pl.pallas_call(
        _pool_kernel,
        grid=(bsz, seq // ts),
        in_specs=[pl.BlockSpec((1, ts, d), lambda b, s: (b, s, 0)),
                  _resident(gains.shape), _resident(w_in.shape), _resident(w_group.shape),
                  _resident(scale.shape), _resident(w_out.shape)],
        out_specs=pl.BlockSpec((1, ts, d), lambda b, s: (b, s, 0)),
        out_shape=jax.ShapeDtypeStruct(h.shape, F32),
        scratch_shapes=[pltpu.VMEM((POOL_WINDOWS[-1], d), F32), pltpu.VMEM((ts, d), BF16)],
        compiler_params=_params(2),
        name="pool",
    )(h, gains, w_in, w_group, scale, w_out)


def kernel(x, norm_gains, ffn1_w_in, ffn1_w_out, ffn2_w_in, ffn2_w_out, hgrn_w_in, hgrn_gnorm,
           hgrn_w_out, hgrn_lb_raw, diff_w_in, diff_lambda, diff_subln, diff_w_out, pool_w_in,
           pool_w_group, pool_scale, pool_w_out):
    bsz, seq, d = x.shape
    depth = norm_gains.shape[0]
    n_mixers = 3
    bf = lambda w: w.astype(BF16)
    h = x
    for i in range(depth):
        kind, j = i % n_mixers, i // n_mixers
        gains = norm_gains[i]
        h = _ffn(h.reshape(bsz * seq, d), gains, bf(ffn1_w_in[i]), bf(ffn1_w_out[i]), 0, 1)
        h = h.reshape(bsz, seq, d)
        if kind == 0:
            h = _hgrn_layer(h, gains, bf(hgrn_w_in[j]), hgrn_gnorm[j][None, :], bf(hgrn_w_out[j]),
                            hgrn_lb_raw, i)
        elif kind == 1:
            h = _attn_layer(h, gains, bf(diff_w_in[j]), diff_lambda[j], diff_subln[j][None, :],
                            bf(diff_w_out[j]), i)
        else:
            h = _pool_layer(h, gains, bf(pool_w_in[j]), bf(pool_w_group[j]), pool_scale[j][None, :],
                            bf(pool_w_out[j]))
        h = _ffn(h.reshape(bsz * seq, d), gains, bf(ffn2_w_in[i]), bf(ffn2_w_out[i]), 4, 5)
        h = h.reshape(bsz, seq, d)
    return h
```

```python
import functools
import math

import jax
import jax.numpy as jnp
from jax import lax
from jax.experimental import pallas as pl
from jax.experimental.pallas import tpu as pltpu

F32 = jnp.float32
BF16 = jnp.bfloat16

NORM_EPS = 1e-6
CHUNK = 64
HG_HEADS = 8
HG_SUB = 16
HG_MID = HG_SUB // 2
DA_HEADS = 8
DA_HEAD_DIM = 64
POOL_WINDOWS = (2, 4, 8, 16)

LANES = 128
BF16_ROWS = 16
V7X_VMEM_BYTES = 64 * 2**20
VMEM_LIMIT = V7X_VMEM_BYTES * 3 // 4

FFN_ROWS = 1024
PROJ_ROWS = 1024
PROJ_SUB_ROWS = 256
FFN_SUB_ROWS = 256
FFN_COLS = 256
MIX_ROWS = 512
HG_CHUNK = 64
HG_SUB_ROWS = 256
ATT_TILE = 512
ATT_HEADS_PER_STEP = 2
NEG_BIG = -1e30
LOG2E = math.log2(math.e)


def _rmsnorm(x, w):
    ms = jnp.mean(x * x, axis=-1, keepdims=True)
    return x * lax.rsqrt(ms + NORM_EPS) * w


def _dot(a, b):
    return jnp.dot(a, b, preferred_element_type=F32)


def _dot_nt(a, b):
    return lax.dot_general(a, b, (((1,), (1,)), ((), ())), preferred_element_type=F32)


def _dot_tn(a, b):
    return lax.dot_general(a, b, (((0,), (0,)), ((), ())), preferred_element_type=F32)


def _silu(x):
    return x * jax.nn.sigmoid(x)


def _resident(shape):
    zeros = (0,) * len(shape)
    return pl.BlockSpec(shape, lambda *_: zeros, pipeline_mode=pl.Buffered(1))


def _params(n_grid):
    return pltpu.CompilerParams(
        dimension_semantics=("arbitrary",) * n_grid, vmem_limit_bytes=VMEM_LIMIT)


def _ffn_kernel(h_ref, gains_ref, wi_ref, wo_ref, o_ref, act_ref, *, d_ff, pre, post):
    for r0 in range(0, h_ref.shape[0], FFN_SUB_ROWS):
        rows = slice(r0, r0 + FFN_SUB_ROWS)
        x = h_ref[rows, :]
        xn = _rmsnorm(x, gains_ref[pre:pre + 1, :]).astype(BF16)
        for lo in range(0, d_ff, FFN_COLS):
            g = _dot(xn, wi_ref[:, lo:lo + FFN_COLS])
            u = _dot(xn, wi_ref[:, d_ff + lo:d_ff + lo + FFN_COLS])
            act_ref[rows, lo:lo + FFN_COLS] = (_silu(g) * u).astype(BF16)
        y = _dot(act_ref[rows, :], wo_ref[...])
        o_ref[rows, :] = x + 0.5 * _rmsnorm(y, gains_ref[post:post + 1, :])


def _ffn(h2, gains, w_in, w_out, pre, post):
    n, d = h2.shape
    d_ff = w_out.shape[0]
    return pl.pallas_call(
        functools.partial(_ffn_kernel, d_ff=d_ff, pre=pre, post=post),
        grid=(n // FFN_ROWS,),
        in_specs=[pl.BlockSpec((FFN_ROWS, d), lambda i: (i, 0)),
                  _resident(gains.shape), _resident(w_in.shape), _resident(w_out.shape)],
        out_specs=pl.BlockSpec((FFN_ROWS, d), lambda i: (i, 0)),
        out_shape=jax.ShapeDtypeStruct((n, d), F32),
        scratch_shapes=[pltpu.VMEM((FFN_ROWS, d_ff), BF16)],
        compiler_params=_params(1),
        name="ffn",
    )(h2, gains, w_in, w_out)


def _hgrn_kernel(h_ref, gains_ref, wi_ref, gn_ref, wo_ref, lb_ref, o_ref,
                 q_s, k_s, v_s, b_s, gate_s, y_s, st_s, *, layer):
    ts, d = h_ref.shape[1], h_ref.shape[2]
    dh = d // HG_HEADS
    n_sub = HG_CHUNK // HG_SUB

    @pl.when(pl.program_id(1) == 0)
    def _():
        st_s[...] = jnp.zeros_like(st_s)

    raw = lb_ref[...]
    e = jnp.exp(raw - jnp.max(raw, axis=0, keepdims=True))
    p = e / jnp.sum(e, axis=0, keepdims=True)
    lb = jnp.sum(p[0:layer + 1, :], axis=0, keepdims=True) - p[0:1, :]

    row_in_chunk = lax.broadcasted_iota(jnp.int32, (HG_SUB_ROWS, d), 0) & (HG_CHUNK - 1)
    tri_r = lax.broadcasted_iota(jnp.int32, (HG_CHUNK, HG_CHUNK), 0)
    tri_c = lax.broadcasted_iota(jnp.int32, (HG_CHUNK, HG_CHUNK), 1)
    causal = tri_c <= tri_r
    gn = gn_ref[...]
    zero_blk = jnp.zeros((HG_SUB, dh), BF16)

    def project(rows):
        x = h_ref[0, rows, :]
        hn = _rmsnorm(x, gains_ref[2:3, :]).astype(BF16)
        q_s[rows, :] = _dot(hn, wi_ref[:, 0:d])
        f = lb + (1.0 - lb) * jax.nn.sigmoid(_dot(hn, wi_ref[:, d:2 * d]))
        k_s[rows, :] = 1.0 - f
        b = jnp.log(f)
        shift = 1
        while shift < HG_CHUNK:
            b = b + jnp.where(row_in_chunk >= shift, pltpu.roll(b, shift, axis=0), 0.0)
            shift *= 2
        b_s[rows, :] = b
        v_s[rows, :] = _dot(hn, wi_ref[:, 2 * d:3 * d])
        gate_s[rows, :] = _silu(_dot(hn, wi_ref[:, 3 * d:4 * d]))
        return x

    def sub(x, i):
        return x[i * HG_SUB:(i + 1) * HG_SUB, :]

    def prepare(rows, hd):
        cols = slice(hd * dh, (hd + 1) * dh)
        q = q_s[rows, cols]
        k = k_s[rows, cols]
        bc = b_s[rows, cols]
        b_last = bc[HG_CHUNK - 1:HG_CHUNK, :]
        q_inter = (q * jnp.exp(bc)).astype(BF16)
        k_state = (k * jnp.exp(b_last - bc)).astype(BF16)
        mids = [bc[j * HG_SUB + HG_MID:j * HG_SUB + HG_MID + 1, :] for j in range(n_sub)]
        mid_row = jnp.concatenate([jnp.broadcast_to(m, (HG_SUB, dh)) for m in mids], axis=0)
        q_r = q * jnp.exp(bc - mid_row)
        k_r = (k * jnp.exp(mid_row - bc)).astype(BF16)
        q_rb = q_r.astype(BF16)
        lhs, rhs = [], []
        for j in range(n_sub):
            lhs_blocks, rhs_blocks = [], []
            for i in range(n_sub):
                if i < j:
                    lhs_blocks.append(zero_blk)
                elif i == j:
                    lhs_blocks.append(sub(q_rb, i))
                else:
                    lhs_blocks.append((sub(q_r, i) * jnp.exp(mids[i] - mids[j])).astype(BF16))
                rhs_blocks.append(sub(k_r, i) if i == j else zero_blk)
            lhs.append(jnp.concatenate(lhs_blocks, axis=0))
            rhs.append(jnp.concatenate(rhs_blocks, axis=0))
        return dict(cols=cols, q_inter=q_inter, k_state=k_state, decay=jnp.exp(b_last),
                    lhs=jnp.concatenate(lhs, axis=1), rhs=jnp.concatenate(rhs, axis=1),
                    v=v_s[rows, cols].astype(BF16))

    def mix_chunk(rows):
        ops = [prepare(rows, hd) for hd in range(HG_HEADS)]
        scores = [_dot_nt(op["lhs"], op["rhs"]) for op in ops]
        outs = []
        for hd, op in enumerate(ops):
            a = jnp.where(causal, scores[hd], 0.0).astype(BF16)
            st = st_s[hd]
            outs.append(_dot_nt(op["q_inter"], st.astype(BF16)) + _dot(a, op["v"]))
            st_s[hd] = st * op["decay"] + _dot_tn(op["v"], op["k_state"])
        for hd, op in enumerate(ops):
            o = outs[hd]
            ms = jnp.mean(o * o, axis=-1, keepdims=True)
            y = o * lax.rsqrt(ms + NORM_EPS) * gn * gate_s[rows, op["cols"]]
            y_s[rows, op["cols"]] = y.astype(BF16)

    for r0 in range(0, ts, HG_SUB_ROWS):
        rows = slice(r0, r0 + HG_SUB_ROWS)
        x = project(rows)
        for c0 in range(r0, r0 + HG_SUB_ROWS, HG_CHUNK):
            mix_chunk(slice(c0, c0 + HG_CHUNK))
        m = _dot(y_s[rows, :], wo_ref[...])
        o_ref[0, rows, :] = x + _rmsnorm(m, gains_ref[3:4, :])


def _hgrn_layer(h, gains, w_in, gnorm, w_out, lb_raw, layer):
    bsz, seq, d = h.shape
    ts = MIX_ROWS
    dh = d // HG_HEADS
    big = pltpu.VMEM((ts, d), F32)
    return pl.pallas_call(
        functools.partial(_hgrn_kernel, layer=layer),
        grid=(bsz, seq // ts),
        in_specs=[pl.BlockSpec((1, ts, d), lambda b, s: (b, s, 0)),
                  _resident(gains.shape), _resident(w_in.shape), _resident(gnorm.shape),
                  _resident(w_out.shape), _resident(lb_raw.shape)],
        out_specs=pl.BlockSpec((1, ts, d), lambda b, s: (b, s, 0)),
        out_shape=jax.ShapeDtypeStruct(h.shape, F32),
        scratch_shapes=[big, big, big, big, big, pltpu.VMEM((ts, d), BF16),
                        pltpu.VMEM((HG_HEADS, dh, dh), F32)],
        compiler_params=_params(2),
        name="hgrn",
    )(h, gains, w_in, gnorm, w_out, lb_raw)


def _qkv_kernel(h_ref, gains_ref, wi_ref, qt_ref, k_ref, vt_ref):
    d = h_ref.shape[2]
    for r0 in range(0, h_ref.shape[1], PROJ_SUB_ROWS):
        rows = slice(r0, r0 + PROJ_SUB_ROWS)
        hn = _rmsnorm(h_ref[0, rows, :], gains_ref[2:3, :]).astype(BF16)
        q = _dot(hn, wi_ref[:, 0:d]) * (DA_HEAD_DIM ** -0.5 * LOG2E)
        qt_ref[0, :, rows] = q.astype(BF16).T
        k_ref[0, rows, :] = _dot(hn, wi_ref[:, d:2 * d]).astype(BF16)
        vt_ref[0, :, rows] = _dot(hn, wi_ref[:, 2 * d:3 * d]).astype(BF16).T


def _qkv(h, gains, w_in):
    bsz, seq, d = h.shape
    ts = PROJ_ROWS
    rows = pl.BlockSpec((1, ts, d), lambda b, s: (b, s, 0))
    cols = pl.BlockSpec((1, d, ts), lambda b, s: (b, 0, s))
    return pl.pallas_call(
        _qkv_kernel,
        grid=(bsz, seq // ts),
        in_specs=[rows, _resident(gains.shape), _resident(w_in.shape)],
        out_specs=[cols, rows, cols],
        out_shape=[jax.ShapeDtypeStruct((bsz, d, seq), BF16), jax.ShapeDtypeStruct((bsz, seq, d), BF16),
                   jax.ShapeDtypeStruct((bsz, d, seq), BF16)],
        compiler_params=_params(2),
        name="attn_qkv",
    )(h, gains, w_in)


def _split_bf16(x, parts):
    out = []
    for _ in range(parts):
        hi = x.astype(BF16).astype(F32)
        out.append(hi)
        x = x - hi
    return out


def _flash_kernel(qt_ref, k_ref, vt_ref, lam_ref, sub_ref, y_ref,
                  qx_ref, kf_ref, s_a, s_b, p_a, p_b, acc_ref, diag_ref, *, lam_init):
    t = ATT_TILE
    dh2 = 2 * DA_HEAD_DIM
    n_maps = 2 * ATT_HEADS_PER_STEP
    qi = pl.program_id(2)
    heads = [pl.program_id(1) * ATT_HEADS_PER_STEP + hh for hh in range(ATT_HEADS_PER_STEP)]
    slopes = [jnp.exp2(jnp.full((1, 1), -8.0 / DA_HEADS, F32) * (h + 1).astype(F32)) * LOG2E
              for h in heads]

    low_bits = 4
    n_parts = 3
    frow = lax.broadcasted_iota(jnp.int32, (dh2, t), 0)

    @pl.when(qi == 0)
    def _():
        key_idx = lax.broadcasted_iota(jnp.int32, (t, dh2), 0)
        flane = lax.broadcasted_iota(jnp.int32, (t, dh2), 1)
        feat_k = jnp.where(flane < n_parts, key_idx >> low_bits,
                           jnp.where(flane < 2 * n_parts, key_idx & ((1 << low_bits) - 1), 0))
        kf_ref[...] = feat_k.astype(F32).astype(BF16)
        key = lax.broadcasted_iota(jnp.int32, (t, t), 0)
        qry = lax.broadcasted_iota(jnp.int32, (t, t), 1)
        chunk_bits = int(math.log2(CHUNK))
        allowed = (key >> chunk_bits) <= (qry >> chunk_bits)
        behind = jnp.minimum(qry - key, 0).astype(F32)
        for hh, a in enumerate(slopes):
            feat_q = jnp.zeros((dh2, t), F32)
            for n, part in enumerate(_split_bf16(a, n_parts)):
                feat_q = jnp.where(frow == n, part * (1 << low_bits), feat_q)
                feat_q = jnp.where(frow == n_parts + n, part, feat_q)
            qx_ref[2 * hh, dh2:, :] = feat_q.astype(BF16)
            qx_ref[2 * hh + 1, dh2:, :] = feat_q.astype(BF16)
            diag_ref[hh] = jnp.where(allowed, behind * (2.0 * a), NEG_BIG)

    for hh in range(ATT_HEADS_PER_STEP):
        qt = qt_ref[0, hh * dh2:(hh + 1) * dh2, :]
        zero = jnp.zeros_like(qt)
        qx_ref[2 * hh, :dh2, :] = jnp.where(frow < DA_HEAD_DIM, qt, zero)
        qx_ref[2 * hh + 1, :dh2, :] = jnp.where(frow >= DA_HEAD_DIM, qt, zero)
    p_b[...] = jnp.zeros_like(p_b)
    acc_ref[...] = jnp.zeros_like(acc_ref)

    def scores(i, s_ref):
        k0 = pl.multiple_of(i * t, t)
        tops = []
        for hh in range(ATT_HEADS_PER_STEP):
            k_ext = jnp.concatenate([k_ref[0, pl.ds(k0, t), hh * dh2:(hh + 1) * dh2], kf_ref[...]], axis=1)
            for mp in (2 * hh, 2 * hh + 1):
                st = _dot(k_ext, qx_ref[mp])
                s_ref[mp] = st
                tops.append(jnp.max(st, axis=0, keepdims=True))
        return tuple(tops)

    def softmax(m, st, top, offset):
        m_new = jnp.maximum(m, top + offset)
        alpha = jnp.exp2(m - m_new)
        p = jnp.exp2(st - (m_new - offset))
        return m_new, alpha, p.astype(BF16)

    ones_rows = jnp.ones((acc_ref.shape[1] - dh2, t), BF16)

    def values(i, alphas, p_of_map):
        k0 = pl.multiple_of(jnp.maximum(i, 0) * t, t)
        for hh in range(ATT_HEADS_PER_STEP):
            vt = jnp.concatenate([vt_ref[0, hh * dh2:(hh + 1) * dh2, pl.ds(k0, t)], ones_rows], axis=0)
            for mp in (2 * hh, 2 * hh + 1):
                acc_ref[mp] = alphas[mp] * acc_ref[mp] + _dot(vt, p_of_map(mp))

    def past_step(i, carry, s_cur, s_next, p_cur, p_prev):
        ms, alphas, tops = carry
        next_tops = scores(i + 1, s_next)
        values(i - 1, alphas, lambda mp: p_prev[mp])
        tile_offset = ((i - qi) * t).astype(F32)
        new_ms, new_alphas = [], []
        for mp in range(n_maps):
            m_new, alpha, p = softmax(ms[mp], s_cur[mp], tops[mp], slopes[mp // 2] * tile_offset)
            p_cur[mp] = p
            new_ms.append(m_new)
            new_alphas.append(alpha)
        return tuple(new_ms), tuple(new_alphas), next_tops

    def diagonal_step(carry, s_cur, p_prev):
        ms, alphas, _ = carry
        values(qi - 1, alphas, lambda mp: p_prev[mp])
        ps, new_alphas = [], []
        for mp in range(n_maps):
            st = s_cur[mp] + diag_ref[mp // 2]
            _, alpha, p = softmax(ms[mp], st, jnp.max(st, axis=0, keepdims=True), 0.0)
            ps.append(p)
            new_alphas.append(alpha)
        values(qi, new_alphas, lambda mp: ps[mp])

    def pair(ii, carry):
        carry = past_step(2 * ii, carry, s_a, s_b, p_a, p_b)
        return past_step(2 * ii + 1, carry, s_b, s_a, p_b, p_a)

    first_tops = scores(0, s_a)
    init = ((jnp.full((1, t), NEG_BIG, F32),) * n_maps, (jnp.ones((1, t), F32),) * n_maps, first_tops)
    carry = lax.fori_loop(0, qi // 2, pair, init)

    @pl.when(qi % 2 == 1)
    def _():
        diagonal_step(past_step(qi - 1, carry, s_a, s_b, p_a, p_b), s_b, p_a)

    @pl.when(qi % 2 == 0)
    def _():
        diagonal_step(carry, s_a, p_b)

    lam = lam_ref[...]
    lam_full = (jnp.exp(jnp.sum(lam[0:1] * lam[1:2], axis=-1, keepdims=True))
                - jnp.exp(jnp.sum(lam[2:3] * lam[3:4], axis=-1, keepdims=True)) + lam_init)
    for hh in range(ATT_HEADS_PER_STEP):
        m0, m1 = 2 * hh, 2 * hh + 1
        ot = (acc_ref[m0, 0:dh2, :] / acc_ref[m0, dh2:dh2 + 1, :]
              - lam_full * (acc_ref[m1, 0:dh2, :] / acc_ref[m1, dh2:dh2 + 1, :]))
        ms = jnp.mean(ot * ot, axis=0, keepdims=True)
        yt = ot * lax.rsqrt(ms + NORM_EPS) * sub_ref[...] * (1.0 - lam_init)
        y_ref[0, :, hh * dh2:(hh + 1) * dh2] = yt.T.astype(BF16)


def _flash(qt, k, vt, lam, subln_col, lam_init):
    bsz, seq, d = k.shape
    dh2 = 2 * DA_HEAD_DIM
    t = ATT_TILE
    hps = ATT_HEADS_PER_STEP
    s_buf = pltpu.VMEM((2 * hps, t, t), F32)
    p_buf = pltpu.VMEM((2 * hps, t, t), BF16)
    return pl.pallas_call(
        functools.partial(_flash_kernel, lam_init=lam_init),
        grid=(bsz, DA_HEADS // hps, seq // t),
        in_specs=[pl.BlockSpec((1, hps * dh2, t), lambda b, h, i: (b, h, i)),
                  pl.BlockSpec((1, seq, hps * dh2), lambda b, h, i: (b, 0, h)),
                  pl.BlockSpec((1, hps * dh2, seq), lambda b, h, i: (b, h, 0)),
                  _resident(lam.shape), _resident(subln_col.shape)],
        out_specs=pl.BlockSpec((1, t, hps * dh2), lambda b, h, i: (b, i, h)),
        out_shape=jax.ShapeDtypeStruct((bsz, seq, d), BF16),
        scratch_shapes=[pltpu.VMEM((2 * hps, 2 * dh2, t), BF16), pltpu.VMEM((t, dh2), BF16),
                        s_buf, s_buf, p_buf, p_buf, pltpu.VMEM((2 * hps, dh2 + BF16_ROWS, t), F32),
                        pltpu.VMEM((hps, t, t), F32)],
        compiler_params=_params(3),
        name="attn_flash",
    )(qt, k, vt, lam, subln_col)


def _proj_residual_kernel(h_ref, y_ref, gains_ref, wo_ref, o_ref):
    for r0 in range(0, h_ref.shape[0], PROJ_SUB_ROWS):
        rows = slice(r0, r0 + PROJ_SUB_ROWS)
        m = _dot(y_ref[rows, :], wo_ref[...])
        o_ref[rows, :] = h_ref[rows, :] + _rmsnorm(m, gains_ref[3:4, :])


def _proj_residual(h2, y2, gains, w_out):
    n, d = h2.shape
    tile = pl.BlockSpec((PROJ_ROWS, d), lambda i: (i, 0))
    return pl.pallas_call(
        _proj_residual_kernel,
        grid=(n // PROJ_ROWS,),
        in_specs=[tile, tile, _resident(gains.shape), _resident(w_out.shape)],
        out_specs=tile,
        out_shape=jax.ShapeDtypeStruct((n, d), F32),
        compiler_params=_params(1),
        name="attn_out",
    )(h2, y2, gains, w_out)


def _attn_layer(h, gains, w_in, lam, subln, w_out, layer):
    bsz, seq, d = h.shape
    lam_init = 0.8 - 0.6 * math.exp(-0.3 * layer)
    qt, k, vt = _qkv(h, gains, w_in)
    y = _flash(qt, k, vt, lam, subln.reshape(-1, 1), lam_init)
    out = _proj_residual(h.reshape(bsz * seq, d), y.reshape(bsz * seq, d), gains, w_out)
    return out.reshape(bsz, seq, d)


def _pool_kernel(h_ref, gains_ref, wi_ref, wg_ref, sc_ref, wo_ref, o_ref, carry_s, y_s):
    ts, d = h_ref.shape[1], h_ref.shape[2]
    gd = d // len(POOL_WINDOWS)
    halo = POOL_WINDOWS[-1]
    s_idx = pl.program_id(1)

    @pl.when(s_idx == 0)
    def _():
        carry_s[...] = jnp.zeros_like(carry_s)

    prev_tail = carry_s[...]
    for r0 in range(0, ts, PROJ_SUB_ROWS):
        rows = slice(r0, r0 + PROJ_SUB_ROWS)
        x = h_ref[0, rows, :]
        hn = _rmsnorm(x, gains_ref[2:3, :]).astype(BF16)
        u = _dot(hn, wi_ref[...])
        cur = jnp.concatenate([prev_tail, u], axis=0)
        prev_tail = u[PROJ_SUB_ROWS - halo:, :]
        pos1 = s_idx * ts + r0 + lax.broadcasted_iota(jnp.int32, (PROJ_SUB_ROWS, gd), 0) + 1
        for g, w in enumerate(POOL_WINDOWS):
            cur = cur + pltpu.roll(cur, w // 2, axis=0)
            count = jnp.minimum(pos1, w).astype(F32)
            pooled = cur[halo:, 0:gd] / count - u[:, g * gd:(g + 1) * gd]
            yg = _dot(pooled.astype(BF16), wg_ref[g]) * sc_ref[:, g * gd:(g + 1) * gd]
            y_s[rows, g * gd:(g + 1) * gd] = yg.astype(BF16)
            cur = cur[:, gd:]
        m = _dot(y_s[rows, :], wo_ref[...])
        o_ref[0, rows, :] = x + _rmsnorm(m, gains_ref[3:4, :])
    carry_s[...] = prev_tail


def _pool_layer(h, gains, w_in, w_group, scale, w_out):
    bsz, seq, d = h.shape
    ts = MIX_ROWS
    return pl.pallas_call(
        _pool_kernel,
        grid=(bsz, seq // ts),
        in_specs=[pl.BlockSpec((1, ts, d), lambda b, s: (b, s, 0)),
                  _resident(gains.shape), _resident(w_in.shape), _resident(w_group.shape),
                  _resident(scale.shape), _resident(w_out.shape)],
        out_specs=pl.BlockSpec((1, ts, d), lambda b, s: (b, s, 0)),
        out_shape=jax.ShapeDtypeStruct(h.shape, F32),
        scratch_shapes=[pltpu.VMEM((POOL_WINDOWS[-1], d), F32), pltpu.VMEM((ts, d), BF16)],
        compiler_params=_params(2),
        name="pool",
    )(h, gains, w_in, w_group, scale, w_out)


def kernel(x, norm_gains, ffn1_w_in, ffn1_w_out, ffn2_w_in, ffn2_w_out, hgrn_w_in, hgrn_gnorm,
           hgrn_w_out, hgrn_lb_raw, diff_w_in, diff_lambda, diff_subln, diff_w_out, pool_w_in,
           pool_w_group, pool_scale, pool_w_out):
    bsz, seq, d = x.shape
    depth = norm_gains.shape[0]
    n_mixers = 3
    bf = lambda w: w.astype(BF16)
    h = x
    for i in range(depth):
        kind, j = i % n_mixers, i // n_mixers
        gains = norm_gains[i]
        h = _ffn(h.reshape(bsz * seq, d), gains, bf(ffn1_w_in[i]), bf(ffn1_w_out[i]), 0, 1)
        h = h.reshape(bsz, seq, d)
        if kind == 0:
            h = _hgrn_layer(h, gains, bf(hgrn_w_in[j]), hgrn_gnorm[j][None, :], bf(hgrn_w_out[j]),
                            hgrn_lb_raw, i)
        elif kind == 1:
            h = _attn_layer(h, gains, bf(diff_w_in[j]), diff_lambda[j], diff_subln[j][None, :],
                            bf(diff_w_out[j]), i)
        else:
            h = _pool_layer(h, gains, bf(pool_w_in[j]), bf(pool_w_group[j]), pool_scale[j][None, :],
                            bf(pool_w_out[j]))
        h = _ffn(h.reshape(bsz * seq, d), gains, bf(ffn2_w_in[i]), bf(ffn2_w_out[i]), 4, 5)
        h = h.reshape(bsz, seq, d)
    return h
```

```python
import functools
import math

import jax
import jax.numpy as jnp
from jax import lax
from jax.experimental import pallas as pl
from jax.experimental.pallas import tpu as pltpu

F32 = jnp.float32
BF16 = jnp.bfloat16

NORM_EPS = 1e-6
CHUNK = 64
HG_HEADS = 8
HG_SUB = 16
HG_MID = HG_SUB // 2
DA_HEADS = 8
DA_HEAD_DIM = 64
POOL_WINDOWS = (2, 4, 8, 16)

LANES = 128
BF16_ROWS = 16
V7X_VMEM_BYTES = 64 * 2**20
VMEM_LIMIT = V7X_VMEM_BYTES * 3 // 4

FFN_ROWS = 1024
PROJ_ROWS = 1024
PROJ_SUB_ROWS = 256
FFN_SUB_ROWS = 256
FFN_COLS = 256
MIX_ROWS = 512
HG_CHUNK = 64
HG_SUB_ROWS = 256
ATT_TILE = 512
ATT_HEADS_PER_STEP = 4
NEG_BIG = -1e30
LOG2E = math.log2(math.e)


def _rmsnorm(x, w):
    ms = jnp.mean(x * x, axis=-1, keepdims=True)
    return x * lax.rsqrt(ms + NORM_EPS) * w


def _dot(a, b):
    return jnp.dot(a, b, preferred_element_type=F32)


def _dot_nt(a, b):
    return lax.dot_general(a, b, (((1,), (1,)), ((), ())), preferred_element_type=F32)


def _dot_tn(a, b):
    return lax.dot_general(a, b, (((0,), (0,)), ((), ())), preferred_element_type=F32)


def _silu(x):
    return x * jax.nn.sigmoid(x)


def _resident(shape):
    zeros = (0,) * len(shape)
    return pl.BlockSpec(shape, lambda *_: zeros, pipeline_mode=pl.Buffered(1))


def _params(n_grid, vmem_limit=VMEM_LIMIT):
    return pltpu.CompilerParams(
        dimension_semantics=("arbitrary",) * n_grid, vmem_limit_bytes=vmem_limit)


def _ffn_kernel(h_ref, gains_ref, wi_ref, wo_ref, o_ref, act_ref, *, d_ff, pre, post):
    for r0 in range(0, h_ref.shape[0], FFN_SUB_ROWS):
        rows = slice(r0, r0 + FFN_SUB_ROWS)
        x = h_ref[rows, :]
        xn = _rmsnorm(x, gains_ref[pre:pre + 1, :]).astype(BF16)
        for lo in range(0, d_ff, FFN_COLS):
            g = _dot(xn, wi_ref[:, lo:lo + FFN_COLS])
            u = _dot(xn, wi_ref[:, d_ff + lo:d_ff + lo + FFN_COLS])
            act_ref[rows, lo:lo + FFN_COLS] = (_silu(g) * u).astype(BF16)
        y = _dot(act_ref[rows, :], wo_ref[...])
        o_ref[rows, :] = x + 0.5 * _rmsnorm(y, gains_ref[post:post + 1, :])


def _ffn(h2, gains, w_in, w_out, pre, post):
    n, d = h2.shape
    d_ff = w_out.shape[0]
    return pl.pallas_call(
        functools.partial(_ffn_kernel, d_ff=d_ff, pre=pre, post=post),
        grid=(n // FFN_ROWS,),
        in_specs=[pl.BlockSpec((FFN_ROWS, d), lambda i: (i, 0)),
                  _resident(gains.shape), _resident(w_in.shape), _resident(w_out.shape)],
        out_specs=pl.BlockSpec((FFN_ROWS, d), lambda i: (i, 0)),
        out_shape=jax.ShapeDtypeStruct((n, d), F32),
        scratch_shapes=[pltpu.VMEM((FFN_ROWS, d_ff), BF16)],
        compiler_params=_params(1),
        name="ffn",
    )(h2, gains, w_in, w_out)


def _hgrn_kernel(h_ref, gains_ref, wi_ref, gn_ref, wo_ref, lb_ref, o_ref,
                 q_s, k_s, v_s, b_s, gate_s, y_s, st_s, *, layer):
    ts, d = h_ref.shape[1], h_ref.shape[2]
    dh = d // HG_HEADS
    n_sub = HG_CHUNK // HG_SUB

    @pl.when(pl.program_id(1) == 0)
    def _():
        st_s[...] = jnp.zeros_like(st_s)

    raw = lb_ref[...]
    e = jnp.exp(raw - jnp.max(raw, axis=0, keepdims=True))
    p = e / jnp.sum(e, axis=0, keepdims=True)
    lb = jnp.sum(p[0:layer + 1, :], axis=0, keepdims=True) - p[0:1, :]

    row_in_chunk = lax.broadcasted_iota(jnp.int32, (HG_SUB_ROWS, d), 0) & (HG_CHUNK - 1)
    tri_r = lax.broadcasted_iota(jnp.int32, (HG_CHUNK, HG_CHUNK), 0)
    tri_c = lax.broadcasted_iota(jnp.int32, (HG_CHUNK, HG_CHUNK), 1)
    causal = tri_c <= tri_r
    gn_all = jnp.concatenate([gn_ref[...]] * HG_HEADS, axis=1)
    zero_blk = jnp.zeros((HG_SUB, dh), BF16)

    def project(rows):
        x = h_ref[0, rows, :]
        hn = _rmsnorm(x, gains_ref[2:3, :]).astype(BF16)
        q_s[rows, :] = _dot(hn, wi_ref[:, 0:d])
        f = lb + (1.0 - lb) * jax.nn.sigmoid(_dot(hn, wi_ref[:, d:2 * d]))
        k_s[rows, :] = 1.0 - f
        b = jnp.log(f)
        shift = 1
        while shift < HG_CHUNK:
            b = b + jnp.where(row_in_chunk >= shift, pltpu.roll(b, shift, axis=0), 0.0)
            shift *= 2
        b_s[rows, :] = b * LOG2E
        v_s[rows, :] = _dot(hn, wi_ref[:, 2 * d:3 * d])
        gate_s[rows, :] = _silu(_dot(hn, wi_ref[:, 3 * d:4 * d])) * gn_all
        return x

    def sub(x, i):
        return x[i * HG_SUB:(i + 1) * HG_SUB, :]

    def prepare(rows, hd):
        cols = slice(hd * dh, (hd + 1) * dh)
        q = q_s[rows, cols]
        k = k_s[rows, cols]
        bc = b_s[rows, cols]
        b_last = bc[HG_CHUNK - 1:HG_CHUNK, :]
        mids = [bc[j * HG_SUB + HG_MID:j * HG_SUB + HG_MID + 1, :] for j in range(n_sub)]

        def per_sub_block(vals):
            return jnp.concatenate([jnp.broadcast_to(v, (HG_SUB, dh)) for v in vals], axis=0)

        mid_row = per_sub_block(mids)
        q_r = q * jnp.exp2(bc - mid_row)
        k_rf = k * jnp.exp2(mid_row - bc)
        k_r = k_rf.astype(BF16)
        q_rb = q_r.astype(BF16)
        q_inter = (q_r * per_sub_block([jnp.exp2(m) for m in mids])).astype(BF16)
        k_state = (k_rf * per_sub_block([jnp.exp2(b_last - m) for m in mids])).astype(BF16)
        lhs, rhs = [], []
        for j in range(n_sub):
            lhs_blocks, rhs_blocks = [], []
            for i in range(n_sub):
                if i < j:
                    lhs_blocks.append(zero_blk)
                elif i == j:
                    lhs_blocks.append(sub(q_rb, i))
                else:
                    lhs_blocks.append((sub(q_r, i) * jnp.exp2(mids[i] - mids[j])).astype(BF16))
                rhs_blocks.append(sub(k_r, i) if i == j else zero_blk)
            lhs.append(jnp.concatenate(lhs_blocks, axis=0))
            rhs.append(jnp.concatenate(rhs_blocks, axis=0))
        return dict(cols=cols, q_inter=q_inter, k_state=k_state, decay=jnp.exp2(b_last),
                    lhs=jnp.concatenate(lhs, axis=1), rhs=jnp.concatenate(rhs, axis=1),
                    v=v_s[rows, cols].astype(BF16))

    def mix_chunk(rows):
        ops = [prepare(rows, hd) for hd in range(HG_HEADS)]
        scores = [_dot_nt(op["lhs"], op["rhs"]) for op in ops]
        outs = []
        for hd, op in enumerate(ops):
            a = jnp.where(causal, scores[hd], 0.0).astype(BF16)
            st = st_s[hd]
            outs.append(_dot_nt(op["q_inter"], st.astype(BF16)) + _dot(a, op["v"]))
            st_s[hd] = st * op["decay"] + _dot_tn(op["v"], op["k_state"])
        for hd, op in enumerate(ops):
            o = outs[hd]
            ms = jnp.mean(o * o, axis=-1, keepdims=True)
            y = o * lax.rsqrt(ms + NORM_EPS) * gate_s[rows, op["cols"]]
            y_s[rows, op["cols"]] = y.astype(BF16)

    for r0 in range(0, ts, HG_SUB_ROWS):
        rows = slice(r0, r0 + HG_SUB_ROWS)
        x = project(rows)
        for c0 in range(r0, r0 + HG_SUB_ROWS, HG_CHUNK):
            mix_chunk(slice(c0, c0 + HG_CHUNK))
        m = _dot(y_s[rows, :], wo_ref[...])
        o_ref[0, rows, :] = x + _rmsnorm(m, gains_ref[3:4, :])


def _hgrn_layer(h, gains, w_in, gnorm, w_out, lb_raw, layer):
    bsz, seq, d = h.shape
    ts = MIX_ROWS
    dh = d // HG_HEADS
    big = pltpu.VMEM((ts, d), F32)
    return pl.pallas_call(
        functools.partial(_hgrn_kernel, layer=layer),
        grid=(bsz, seq // ts),
        in_specs=[pl.BlockSpec((1, ts, d), lambda b, s: (b, s, 0)),
                  _resident(gains.shape), _resident(w_in.shape), _resident(gnorm.shape),
                  _resident(w_out.shape), _resident(lb_raw.shape)],
        out_specs=pl.BlockSpec((1, ts, d), lambda b, s: (b, s, 0)),
        out_shape=jax.ShapeDtypeStruct(h.shape, F32),
        scratch_shapes=[big, big, big, big, big, pltpu.VMEM((ts, d), BF16),
                        pltpu.VMEM((HG_HEADS, dh, dh), F32)],
        compiler_params=_params(2),
        name="hgrn",
    )(h, gains, w_in, gnorm, w_out, lb_raw)


def _qkv_kernel(h_ref, gains_ref, wi_ref, qt_ref, k_ref, vt_ref):
    d = h_ref.shape[2]
    for r0 in range(0, h_ref.shape[1], PROJ_SUB_ROWS):
        rows = slice(r0, r0 + PROJ_SUB_ROWS)
        hn = _rmsnorm(h_ref[0, rows, :], gains_ref[2:3, :]).astype(BF16)
        q = _dot(hn, wi_ref[:, 0:d]) * (DA_HEAD_DIM ** -0.5 * LOG2E)
        qt_ref[0, :, rows] = q.astype(BF16).T
        k_ref[0, rows, :] = _dot(hn, wi_ref[:, d:2 * d]).astype(BF16)
        vt_ref[0, :, rows] = _dot(hn, wi_ref[:, 2 * d:3 * d]).astype(BF16).T


def _qkv(h, gains, w_in):
    bsz, seq, d = h.shape
    ts = PROJ_ROWS
    rows = pl.BlockSpec((1, ts, d), lambda b, s: (b, s, 0))
    cols = pl.BlockSpec((1, d, ts), lambda b, s: (b, 0, s))
    return pl.pallas_call(
        _qkv_kernel,
        grid=(bsz, seq // ts),
        in_specs=[rows, _resident(gains.shape), _resident(w_in.shape)],
        out_specs=[cols, rows, cols],
        out_shape=[jax.ShapeDtypeStruct((bsz, d, seq), BF16), jax.ShapeDtypeStruct((bsz, seq, d), BF16),
                   jax.ShapeDtypeStruct((bsz, d, seq), BF16)],
        compiler_params=_params(2),
        name="attn_qkv",
    )(h, gains, w_in)


def _split_bf16(x, parts):
    out = []
    for _ in range(parts):
        hi = x.astype(BF16).astype(F32)
        out.append(hi)
        x = x - hi
    return out


def _flash_kernel(qt_ref, k_ref, vt_ref, lam_ref, sub_ref, y_ref,
                  qx_ref, kf_ref, s_a, s_b, p_a, p_b, acc_ref, diag_ref, *, lam_init):
    t = ATT_TILE
    dh2 = 2 * DA_HEAD_DIM
    n_maps = 2 * ATT_HEADS_PER_STEP
    qi = pl.program_id(2)
    heads = [pl.program_id(1) * ATT_HEADS_PER_STEP + hh for hh in range(ATT_HEADS_PER_STEP)]
    slopes = [jnp.exp2(jnp.full((1, 1), -8.0 / DA_HEADS, F32) * (h + 1).astype(F32)) * LOG2E
              for h in heads]

    low_bits = 4
    n_parts = 3
    frow = lax.broadcasted_iota(jnp.int32, (dh2, t), 0)

    @pl.when(qi == 0)
    def _():
        key_idx = lax.broadcasted_iota(jnp.int32, (t, dh2), 0)
        flane = lax.broadcasted_iota(jnp.int32, (t, dh2), 1)
        feat_k = jnp.where(flane < n_parts, key_idx >> low_bits,
                           jnp.where(flane < 2 * n_parts, key_idx & ((1 << low_bits) - 1), 0))
        kf_ref[...] = feat_k.astype(F32).astype(BF16)
        key = lax.broadcasted_iota(jnp.int32, (t, t), 0)
        qry = lax.broadcasted_iota(jnp.int32, (t, t), 1)
        chunk_bits = int(math.log2(CHUNK))
        allowed = (key >> chunk_bits) <= (qry >> chunk_bits)
        diag_ref[...] = jnp.where(allowed, jnp.minimum(qry - key, 0).astype(F32), -1e32)
        for hh, a in enumerate(slopes):
            feat_q = jnp.zeros((dh2, t), F32)
            for n, part in enumerate(_split_bf16(a, n_parts)):
                feat_q = jnp.where(frow == n, part * (1 << low_bits), feat_q)
                feat_q = jnp.where(frow == n_parts + n, part, feat_q)
            qx_ref[2 * hh, dh2:, :] = feat_q.astype(BF16)
            qx_ref[2 * hh + 1, dh2:, :] = feat_q.astype(BF16)

    for hh in range(ATT_HEADS_PER_STEP):
        qt = qt_ref[0, hh * dh2:(hh + 1) * dh2, :]
        zero = jnp.zeros_like(qt)
        qx_ref[2 * hh, :dh2, :] = jnp.where(frow < DA_HEAD_DIM, qt, zero)
        qx_ref[2 * hh + 1, :dh2, :] = jnp.where(frow >= DA_HEAD_DIM, qt, zero)
    p_b[...] = jnp.zeros_like(p_b)
    acc_ref[...] = jnp.zeros_like(acc_ref)

    def scores(i, s_ref):
        k0 = pl.multiple_of(i * t, t)
        tops = []
        for hh in range(ATT_HEADS_PER_STEP):
            k_ext = jnp.concatenate([k_ref[0, pl.ds(k0, t), hh * dh2:(hh + 1) * dh2], kf_ref[...]], axis=1)
            for mp in (2 * hh, 2 * hh + 1):
                st = _dot(k_ext, qx_ref[mp])
                s_ref[mp] = st
                tops.append(jnp.max(st, axis=0, keepdims=True))
        return tuple(tops)

    def softmax(m, st, top, offset):
        m_new = jnp.maximum(m, top + offset)
        alpha = jnp.exp2(m - m_new)
        p = jnp.exp2(st - (m_new - offset))
        return m_new, alpha, p.astype(BF16)

    ones_rows = jnp.ones((acc_ref.shape[1] - dh2, t), BF16)

    def values(i, alphas, p_of_map):
        k0 = pl.multiple_of(jnp.maximum(i, 0) * t, t)
        for hh in range(ATT_HEADS_PER_STEP):
            vt = jnp.concatenate([vt_ref[0, hh * dh2:(hh + 1) * dh2, pl.ds(k0, t)], ones_rows], axis=0)
            for mp in (2 * hh, 2 * hh + 1):
                acc_ref[mp] = alphas[mp] * acc_ref[mp] + _dot(vt, p_of_map(mp))

    def past_step(i, carry, s_cur, s_next, p_cur, p_prev):
        ms, alphas, tops = carry
        next_tops = scores(i + 1, s_next)
        values(i - 1, alphas, lambda mp: p_prev[mp])
        tile_offset = ((i - qi) * t).astype(F32)
        new_ms, new_alphas = [], []
        for mp in range(n_maps):
            m_new, alpha, p = softmax(ms[mp], s_cur[mp], tops[mp], slopes[mp // 2] * tile_offset)
            p_cur[mp] = p
            new_ms.append(m_new)
            new_alphas.append(alpha)
        return tuple(new_ms), tuple(new_alphas), next_tops

    def diagonal_step(carry, s_cur, p_prev):
        ms, alphas, _ = carry
        values(qi - 1, alphas, lambda mp: p_prev[mp])
        ps, new_alphas = [], []
        for mp in range(n_maps):
            two_a = 2.0 * slopes[mp // 2]
            blocks = []
            for c0 in range(0, t, LANES):
                lanes = slice(c0, c0 + LANES)
                blocks.append(s_cur[mp, 0:c0 + LANES, lanes] + diag_ref[0:c0 + LANES, lanes] * two_a)
            top = jnp.concatenate([jnp.max(b, axis=0, keepdims=True) for b in blocks], axis=1)
            m_new = jnp.maximum(ms[mp], top)
            p_blocks = []
            for c0, b in zip(range(0, t, LANES), blocks):
                p_blk = jnp.exp2(b - m_new[:, c0:c0 + LANES]).astype(BF16)
                if c0 + LANES < t:
                    p_blk = jnp.concatenate([p_blk, jnp.zeros((t - c0 - LANES, LANES), BF16)], axis=0)
                p_blocks.append(p_blk)
            ps.append(jnp.concatenate(p_blocks, axis=1))
            new_alphas.append(jnp.exp2(ms[mp] - m_new))
        values(qi, new_alphas, lambda mp: ps[mp])

    def pair(ii, carry):
        carry = past_step(2 * ii, carry, s_a, s_b, p_a, p_b)
        return past_step(2 * ii + 1, carry, s_b, s_a, p_b, p_a)

    first_tops = scores(0, s_a)
    init = ((jnp.full((1, t), NEG_BIG, F32),) * n_maps, (jnp.ones((1, t), F32),) * n_maps, first_tops)
    carry = lax.fori_loop(0, qi // 2, pair, init)

    @pl.when(qi % 2 == 1)
    def _():
        diagonal_step(past_step(qi - 1, carry, s_a, s_b, p_a, p_b), s_b, p_a)

    @pl.when(qi % 2 == 0)
    def _():
        diagonal_step(carry, s_a, p_b)

    lam = lam_ref[...]
    lam_full = (jnp.exp(jnp.sum(lam[0:1] * lam[1:2], axis=-1, keepdims=True))
                - jnp.exp(jnp.sum(lam[2:3] * lam[3:4], axis=-1, keepdims=True)) + lam_init)
    for hh in range(ATT_HEADS_PER_STEP):
        m0, m1 = 2 * hh, 2 * hh + 1
        ot = (acc_ref[m0, 0:dh2, :] / acc_ref[m0, dh2:dh2 + 1, :]
              - lam_full * (acc_ref[m1, 0:dh2, :] / acc_ref[m1, dh2:dh2 + 1, :]))
        ms = jnp.mean(ot * ot, axis=0, keepdims=True)
        yt = ot * lax.rsqrt(ms + NORM_EPS) * sub_ref[...] * (1.0 - lam_init)
        y_ref[0, :, hh * dh2:(hh + 1) * dh2] = yt.T.astype(BF16)


def _flash(qt, k, vt, lam, subln_col, lam_init):
    bsz, seq, d = k.shape
    dh2 = 2 * DA_HEAD_DIM
    t = ATT_TILE
    hps = ATT_HEADS_PER_STEP
    s_buf = pltpu.VMEM((2 * hps, t, t), F32)
    p_buf = pltpu.VMEM((2 * hps, t, t), BF16)
    return pl.pallas_call(
        functools.partial(_flash_kernel, lam_init=lam_init),
        grid=(bsz, DA_HEADS // hps, seq // t),
        in_specs=[pl.BlockSpec((1, hps * dh2, t), lambda b, h, i: (b, h, i)),
                  pl.BlockSpec((1, seq, hps * dh2), lambda b, h, i: (b, 0, h), pipeline_mode=pl.Buffered(1)),
                  pl.BlockSpec((1, hps * dh2, seq), lambda b, h, i: (b, h, 0), pipeline_mode=pl.Buffered(1)),
                  _resident(lam.shape), _resident(subln_col.shape)],
        out_specs=pl.BlockSpec((1, t, hps * dh2), lambda b, h, i: (b, i, h)),
        out_shape=jax.ShapeDtypeStruct((bsz, seq, d), BF16),
        scratch_shapes=[pltpu.VMEM((2 * hps, 2 * dh2, t), BF16), pltpu.VMEM((t, dh2), BF16),
                        s_buf, s_buf, p_buf, p_buf, pltpu.VMEM((2 * hps, dh2 + BF16_ROWS, t), F32),
                        pltpu.VMEM((t, t), F32)],
        compiler_params=_params(3, V7X_VMEM_BYTES * 15 // 16),
        name="attn_flash",
    )(qt, k, vt, lam, subln_col)


def _proj_residual_kernel(h_ref, y_ref, gains_ref, wo_ref, o_ref):
    for r0 in range(0, h_ref.shape[0], PROJ_SUB_ROWS):
        rows = slice(r0, r0 + PROJ_SUB_ROWS)
        m = _dot(y_ref[rows, :], wo_ref[...])
        o_ref[rows, :] = h_ref[rows, :] + _rmsnorm(m, gains_ref[3:4, :])


def _proj_residual(h2, y2, gains, w_out):
    n, d = h2.shape
    tile = pl.BlockSpec((PROJ_ROWS, d), lambda i: (i, 0))
    return pl.pallas_call(
        _proj_residual_kernel,
        grid=(n // PROJ_ROWS,),
        in_specs=[tile, tile, _resident(gains.shape), _resident(w_out.shape)],
        out_specs=tile,
        out_shape=jax.ShapeDtypeStruct((n, d), F32),
        compiler_params=_params(1),
        name="attn_out",
    )(h2, y2, gains, w_out)


def _attn_layer(h, gains, w_in, lam, subln, w_out, layer):
    bsz, seq, d = h.shape
    lam_init = 0.8 - 0.6 * math.exp(-0.3 * layer)
    qt, k, vt = _qkv(h, gains, w_in)
    y = _flash(qt, k, vt, lam, subln.reshape(-1, 1), lam_init)
    out = _proj_residual(h.reshape(bsz * seq, d), y.reshape(bsz * seq, d), gains, w_out)
    return out.reshape(bsz, seq, d)


def _pool_kernel(h_ref, gains_ref, wi_ref, wg_ref, sc_ref, wo_ref, o_ref, carry_s, y_s):
    ts, d = h_ref.shape[1], h_ref.shape[2]
    gd = d // len(POOL_WINDOWS)
    halo = POOL_WINDOWS[-1]
    s_idx = pl.program_id(1)

    @pl.when(s_idx == 0)
    def _():
        carry_s[...] = jnp.zeros_like(carry_s)

    prev_tail = carry_s[...]
    for r0 in range(0, ts, PROJ_SUB_ROWS):
        rows = slice(r0, r0 + PROJ_SUB_ROWS)
        x = h_ref[0, rows, :]
        hn = _rmsnorm(x, gains_ref[2:3, :]).astype(BF16)
        u = _dot(hn, wi_ref[...])
        cur = jnp.concatenate([prev_tail, u], axis=0)
        prev_tail = u[PROJ_SUB_ROWS - halo:, :]
        pos1 = s_idx * ts + r0 + lax.broadcasted_iota(jnp.int32, (PROJ_SUB_ROWS, gd), 0) + 1
        for g, w in enumerate(POOL_WINDOWS):
            cur = cur + pltpu.roll(cur, w // 2, axis=0)
            count = jnp.minimum(pos1, w).astype(F32)
            pooled = cur[halo:, 0:gd] / count - u[:, g * gd:(g + 1) * gd]
            yg = _dot(pooled.astype(BF16), wg_ref[g]) * sc_ref[:, g * gd:(g + 1) * gd]
            y_s[rows, g * gd:(g + 1) * gd] = yg.astype(BF16)
            cur = cur[:, gd:]
        m = _dot(y_s[rows, :], wo_ref[...])
        o_ref[0, rows, :] = x + _rmsnorm(m, gains_ref[3:4, :])
    carry_s[...] = prev_tail


def _pool_layer(h, gains, w_in, w_group, scale, w_out):
    bsz, seq, d = h.shape
    ts = MIX_ROWS
    return pl.pallas_call(
        _pool_kernel,
        grid=(bsz, seq // ts),
        in_specs=[pl.BlockSpec((1, ts, d), lambda b, s: (b, s, 0)),
                  _resident(gains.shape), _resident(w_in.shape), _resident(w_group.shape),
                  _resident(scale.shape), _resident(w_out.shape)],
        out_specs=pl.BlockSpec((1, ts, d), lambda b, s: (b, s, 0)),
        out_shape=jax.ShapeDtypeStruct(h.shape, F32),
        scratch_shapes=[pltpu.VMEM((POOL_WINDOWS[-1], d), F32), pltpu.VMEM((ts, d), BF16)],
        compiler_params=_params(2),
        name="pool",
    )(h, gains, w_in, w_group, scale, w_out)


def kernel(x, norm_gains, ffn1_w_in, ffn1_w_out, ffn2_w_in, ffn2_w_out, hgrn_w_in, hgrn_gnorm,
           hgrn_w_out, hgrn_lb_raw, diff_w_in, diff_lambda, diff_subln, diff_w_out, pool_w_in,
           pool_w_group, pool_scale, pool_w_out):
    bsz, seq, d = x.shape
    depth = norm_gains.shape[0]
    n_mixers = 3
    bf = lambda w: w.astype(BF16)
    h = x
    for i in range(depth):
        kind, j = i % n_mixers, i // n_mixers
        gains = norm_gains[i]
        h = _ffn(h.reshape(bsz * seq, d), gains, bf(ffn1_w_in[i]), bf(ffn1_w_out[i]), 0, 1)
        h = h.reshape(bsz, seq, d)
        if kind == 0:
            h = _hgrn_layer(h, gains, bf(hgrn_w_in[j]), hgrn_gnorm[j][None, :], bf(hgrn_w_out[j]),
                            hgrn_lb_raw, i)
        elif kind == 1:
            h = _attn_layer(h, gains, bf(diff_w_in[j]), diff_lambda[j], diff_subln[j][None, :],
                            bf(diff_w_out[j]), i)
        else:
            h = _pool_layer(h, gains, bf(pool_w_in[j]), bf(pool_w_group[j]), pool_scale[j][None, :],
                            bf(pool_w_out[j]))
        h = _ffn(h.reshape(bsz * seq, d), gains, bf(ffn2_w_in[i]), bf(ffn2_w_out[i]), 4, 5)
        h = h.reshape(bsz, seq, d)
    return h
```

```python
import functools
import math

import jax
import jax.numpy as jnp
from jax import lax
from jax.experimental import pallas as pl
from jax.experimental.pallas import tpu as pltpu

F32 = jnp.float32
BF16 = jnp.bfloat16

NORM_EPS = 1e-6
CHUNK = 64
HG_HEADS = 8
HG_SUB = 16
HG_MID = HG_SUB // 2
DA_HEADS = 8
DA_HEAD_DIM = 64
POOL_WINDOWS = (2, 4, 8, 16)

LANES = 128
BF16_ROWS = 16
V7X_VMEM_BYTES = 64 * 2**20
VMEM_LIMIT = V7X_VMEM_BYTES * 3 // 4
ATT_VMEM_LIMIT = V7X_VMEM_BYTES * 15 // 16

FFN_ROWS = 1024
PROJ_ROWS = 1024
PROJ_SUB_ROWS = 256
FFN_SUB_ROWS = 256
FFN_COLS = 256
MIX_ROWS = 512
HG_CHUNK = 64
HG_SUB_ROWS = 256
ATT_TILE = 512
ATT_HEADS_PER_STEP = 4
NEG_BIG = -1e30
MASKED_DISTANCE = -1e32
LOG2E = math.log2(math.e)


def _rmsnorm(x, w):
    ms = jnp.mean(x * x, axis=-1, keepdims=True)
    return x * lax.rsqrt(ms + NORM_EPS) * w


def _dot(a, b):
    return jnp.dot(a, b, preferred_element_type=F32)


def _dot_nt(a, b):
    return lax.dot_general(a, b, (((1,), (1,)), ((), ())), preferred_element_type=F32)


def _dot_tn(a, b):
    return lax.dot_general(a, b, (((0,), (0,)), ((), ())), preferred_element_type=F32)


def _silu(x):
    return x * jax.nn.sigmoid(x)


def _resident(shape):
    zeros = (0,) * len(shape)
    return pl.BlockSpec(shape, lambda *_: zeros, pipeline_mode=pl.Buffered(1))


def _params(n_grid, vmem_limit=VMEM_LIMIT):
    return pltpu.CompilerParams(
        dimension_semantics=("arbitrary",) * n_grid, vmem_limit_bytes=vmem_limit)


def _ffn_kernel(h_ref, gains_ref, wi_ref, wo_ref, o_ref, act_ref, *, d_ff, pre, post):
    for r0 in range(0, h_ref.shape[0], FFN_SUB_ROWS):
        rows = slice(r0, r0 + FFN_SUB_ROWS)
        x = h_ref[rows, :]
        xn = _rmsnorm(x, gains_ref[pre:pre + 1, :]).astype(BF16)
        for lo in range(0, d_ff, FFN_COLS):
            g = _dot(xn, wi_ref[:, lo:lo + FFN_COLS])
            u = _dot(xn, wi_ref[:, d_ff + lo:d_ff + lo + FFN_COLS])
            act_ref[rows, lo:lo + FFN_COLS] = (_silu(g) * u).astype(BF16)
        y = _dot(act_ref[rows, :], wo_ref[...])
        o_ref[rows, :] = x + 0.5 * _rmsnorm(y, gains_ref[post:post + 1, :])


def _ffn(h2, gains, w_in, w_out, pre, post):
    n, d = h2.shape
    d_ff = w_out.shape[0]
    return pl.pallas_call(
        functools.partial(_ffn_kernel, d_ff=d_ff, pre=pre, post=post),
        grid=(n // FFN_ROWS,),
        in_specs=[pl.BlockSpec((FFN_ROWS, d), lambda i: (i, 0)),
                  _resident(gains.shape), _resident(w_in.shape), _resident(w_out.shape)],
        out_specs=pl.BlockSpec((FFN_ROWS, d), lambda i: (i, 0)),
        out_shape=jax.ShapeDtypeStruct((n, d), F32),
        scratch_shapes=[pltpu.VMEM((FFN_ROWS, d_ff), BF16)],
        compiler_params=_params(1),
        name="ffn",
    )(h2, gains, w_in, w_out)


def _hgrn_kernel(h_ref, gains_ref, wi_ref, gn_ref, wo_ref, lb_ref, o_ref,
                 q_s, k_s, v_s, b_s, gate_s, y_s, st_s, *, layer):
    ts, d = h_ref.shape[1], h_ref.shape[2]
    dh = d // HG_HEADS
    n_sub = HG_CHUNK // HG_SUB

    @pl.when(pl.program_id(1) == 0)
    def _():
        st_s[...] = jnp.zeros_like(st_s)

    raw = lb_ref[...]
    e = jnp.exp(raw - jnp.max(raw, axis=0, keepdims=True))
    p = e / jnp.sum(e, axis=0, keepdims=True)
    lb = jnp.sum(p[0:layer + 1, :], axis=0, keepdims=True) - p[0:1, :]

    row_in_chunk = lax.broadcasted_iota(jnp.int32, (HG_SUB_ROWS, d), 0) & (HG_CHUNK - 1)
    tri_r = lax.broadcasted_iota(jnp.int32, (HG_CHUNK, HG_CHUNK), 0)
    tri_c = lax.broadcasted_iota(jnp.int32, (HG_CHUNK, HG_CHUNK), 1)
    causal = tri_c <= tri_r
    gn_all = jnp.concatenate([gn_ref[...]] * HG_HEADS, axis=1)
    zero_blk = jnp.zeros((HG_SUB, dh), BF16)

    def project(rows):
        x = h_ref[0, rows, :]
        hn = _rmsnorm(x, gains_ref[2:3, :]).astype(BF16)
        q_s[rows, :] = _dot(hn, wi_ref[:, 0:d])
        f = lb + (1.0 - lb) * jax.nn.sigmoid(_dot(hn, wi_ref[:, d:2 * d]))
        k_s[rows, :] = 1.0 - f
        b = jnp.log(f)
        shift = 1
        while shift < HG_CHUNK:
            b = b + jnp.where(row_in_chunk >= shift, pltpu.roll(b, shift, axis=0), 0.0)
            shift *= 2
        b_s[rows, :] = b * LOG2E
        v_s[rows, :] = _dot(hn, wi_ref[:, 2 * d:3 * d])
        gate_s[rows, :] = _silu(_dot(hn, wi_ref[:, 3 * d:4 * d])) * gn_all
        return x

    def sub(x, i):
        return x[i * HG_SUB:(i + 1) * HG_SUB, :]

    def prepare(rows, hd):
        cols = slice(hd * dh, (hd + 1) * dh)
        q = q_s[rows, cols]
        k = k_s[rows, cols]
        bc = b_s[rows, cols]
        b_last = bc[HG_CHUNK - 1:HG_CHUNK, :]
        mids = [bc[j * HG_SUB + HG_MID:j * HG_SUB + HG_MID + 1, :] for j in range(n_sub)]

        def per_sub_block(vals):
            return jnp.concatenate([jnp.broadcast_to(v, (HG_SUB, dh)) for v in vals], axis=0)

        mid_row = per_sub_block(mids)
        q_r = q * jnp.exp2(bc - mid_row)
        k_rf = k * jnp.exp2(mid_row - bc)
        k_r = k_rf.astype(BF16)
        q_rb = q_r.astype(BF16)
        q_inter = (q_r * per_sub_block([jnp.exp2(m) for m in mids])).astype(BF16)
        k_state = (k_rf * per_sub_block([jnp.exp2(b_last - m) for m in mids])).astype(BF16)
        lhs, rhs = [], []
        for j in range(n_sub):
            lhs_blocks, rhs_blocks = [], []
            for i in range(n_sub):
                if i < j:
                    lhs_blocks.append(zero_blk)
                elif i == j:
                    lhs_blocks.append(sub(q_rb, i))
                else:
                    lhs_blocks.append((sub(q_r, i) * jnp.exp2(mids[i] - mids[j])).astype(BF16))
                rhs_blocks.append(sub(k_r, i) if i == j else zero_blk)
            lhs.append(jnp.concatenate(lhs_blocks, axis=0))
            rhs.append(jnp.concatenate(rhs_blocks, axis=0))
        return dict(cols=cols, q_inter=q_inter, k_state=k_state, decay=jnp.exp2(b_last),
                    lhs=jnp.concatenate(lhs, axis=1), rhs=jnp.concatenate(rhs, axis=1),
                    v=v_s[rows, cols].astype(BF16))

    def mix_chunk(rows):
        ops = [prepare(rows, hd) for hd in range(HG_HEADS)]
        scores = [_dot_nt(op["lhs"], op["rhs"]) for op in ops]
        outs = []
        for hd, op in enumerate(ops):
            a = jnp.where(causal, scores[hd], 0.0).astype(BF16)
            st = st_s[hd]
            outs.append(_dot_nt(op["q_inter"], st.astype(BF16)) + _dot(a, op["v"]))
            st_s[hd] = st * op["decay"] + _dot_tn(op["v"], op["k_state"])
        for hd, op in enumerate(ops):
            o = outs[hd]
            ms = jnp.mean(o * o, axis=-1, keepdims=True)
            y = o * lax.rsqrt(ms + NORM_EPS) * gate_s[rows, op["cols"]]
            y_s[rows, op["cols"]] = y.astype(BF16)

    for r0 in range(0, ts, HG_SUB_ROWS):
        rows = slice(r0, r0 + HG_SUB_ROWS)
        x = project(rows)
        for c0 in range(r0, r0 + HG_SUB_ROWS, HG_CHUNK):
            mix_chunk(slice(c0, c0 + HG_CHUNK))
        m = _dot(y_s[rows, :], wo_ref[...])
        o_ref[0, rows, :] = x + _rmsnorm(m, gains_ref[3:4, :])


def _hgrn_layer(h, gains, w_in, gnorm, w_out, lb_raw, layer):
    bsz, seq, d = h.shape
    ts = MIX_ROWS
    dh = d // HG_HEADS
    big = pltpu.VMEM((ts, d), F32)
    return pl.pallas_call(
        functools.partial(_hgrn_kernel, layer=layer),
        grid=(bsz, seq // ts),
        in_specs=[pl.BlockSpec((1, ts, d), lambda b, s: (b, s, 0)),
                  _resident(gains.shape), _resident(w_in.shape), _resident(gnorm.shape),
                  _resident(w_out.shape), _resident(lb_raw.shape)],
        out_specs=pl.BlockSpec((1, ts, d), lambda b, s: (b, s, 0)),
        out_shape=jax.ShapeDtypeStruct(h.shape, F32),
        scratch_shapes=[big, big, big, big, big, pltpu.VMEM((ts, d), BF16),
                        pltpu.VMEM((HG_HEADS, dh, dh), F32)],
        compiler_params=_params(2),
        name="hgrn",
    )(h, gains, w_in, gnorm, w_out, lb_raw)


def _qkv_kernel(h_ref, gains_ref, wi_ref, qt_ref, k_ref, vt_ref):
    d = h_ref.shape[2]
    for r0 in range(0, h_ref.shape[1], PROJ_SUB_ROWS):
        rows = slice(r0, r0 + PROJ_SUB_ROWS)
        hn = _rmsnorm(h_ref[0, rows, :], gains_ref[2:3, :]).astype(BF16)
        q = _dot(hn, wi_ref[:, 0:d]) * (DA_HEAD_DIM ** -0.5 * LOG2E)
        qt_ref[0, :, rows] = q.astype(BF16).T
        k_ref[0, rows, :] = _dot(hn, wi_ref[:, d:2 * d]).astype(BF16)
        vt_ref[0, :, rows] = _dot(hn, wi_ref[:, 2 * d:3 * d]).astype(BF16).T


def _qkv(h, gains, w_in):
    bsz, seq, d = h.shape
    ts = PROJ_ROWS
    rows = pl.BlockSpec((1, ts, d), lambda b, s: (b, s, 0))
    cols = pl.BlockSpec((1, d, ts), lambda b, s: (b, 0, s))
    return pl.pallas_call(
        _qkv_kernel,
        grid=(bsz, seq // ts),
        in_specs=[rows, _resident(gains.shape), _resident(w_in.shape)],
        out_specs=[cols, rows, cols],
        out_shape=[jax.ShapeDtypeStruct((bsz, d, seq), BF16), jax.ShapeDtypeStruct((bsz, seq, d), BF16),
                   jax.ShapeDtypeStruct((bsz, d, seq), BF16)],
        compiler_params=_params(2),
        name="attn_qkv",
    )(h, gains, w_in)


def _split_bf16(x, parts):
    out = []
    for _ in range(parts):
        hi = x.astype(BF16).astype(F32)
        out.append(hi)
        x = x - hi
    return out


def _flash_kernel(qt_ref, k_ref, vt_ref, lam_ref, sub_ref, y_ref,
                  qx_ref, kf_ref, s_a, s_b, p_a, p_b, acc_ref, diag_ref, *, lam_init):
    t = ATT_TILE
    dh2 = 2 * DA_HEAD_DIM
    n_maps = 2 * ATT_HEADS_PER_STEP
    qi = pl.program_id(2)
    heads = [pl.program_id(1) * ATT_HEADS_PER_STEP + hh for hh in range(ATT_HEADS_PER_STEP)]
    slopes = [jnp.exp2(jnp.full((1, 1), -8.0 / DA_HEADS, F32) * (h + 1).astype(F32)) * LOG2E
              for h in heads]

    low_bits = 4
    n_parts = 3
    frow = lax.broadcasted_iota(jnp.int32, (dh2, t), 0)

    @pl.when(qi == 0)
    def _():
        key_idx = lax.broadcasted_iota(jnp.int32, (t, dh2), 0)
        flane = lax.broadcasted_iota(jnp.int32, (t, dh2), 1)
        feat_k = jnp.where(flane < n_parts, key_idx >> low_bits,
                           jnp.where(flane < 2 * n_parts, key_idx & ((1 << low_bits) - 1), 0))
        kf_ref[...] = feat_k.astype(F32).astype(BF16)
        key = lax.broadcasted_iota(jnp.int32, (t, t), 0)
        qry = lax.broadcasted_iota(jnp.int32, (t, t), 1)
        chunk_bits = int(math.log2(CHUNK))
        allowed = (key >> chunk_bits) <= (qry >> chunk_bits)
        diag_ref[...] = jnp.where(allowed, jnp.minimum(qry - key, 0).astype(F32), MASKED_DISTANCE)
        for hh, a in enumerate(slopes):
            feat_q = jnp.zeros((dh2, t), F32)
            for n, part in enumerate(_split_bf16(a, n_parts)):
                feat_q = jnp.where(frow == n, part * (1 << low_bits), feat_q)
                feat_q = jnp.where(frow == n_parts + n, part, feat_q)
            qx_ref[2 * hh, dh2:, :] = feat_q.astype(BF16)
            qx_ref[2 * hh + 1, dh2:, :] = feat_q.astype(BF16)

    for hh in range(ATT_HEADS_PER_STEP):
        qt = qt_ref[0, hh * dh2:(hh + 1) * dh2, :]
        zero = jnp.zeros_like(qt)
        qx_ref[2 * hh, :dh2, :] = jnp.where(frow < DA_HEAD_DIM, qt, zero)
        qx_ref[2 * hh + 1, :dh2, :] = jnp.where(frow >= DA_HEAD_DIM, qt, zero)
    p_b[...] = jnp.zeros_like(p_b)
    acc_ref[...] = jnp.zeros_like(acc_ref)

    def scores(i, s_ref):
        k0 = pl.multiple_of(i * t, t)
        tops = []
        for hh in range(ATT_HEADS_PER_STEP):
            k_ext = jnp.concatenate([k_ref[0, pl.ds(k0, t), hh * dh2:(hh + 1) * dh2], kf_ref[...]], axis=1)
            for mp in (2 * hh, 2 * hh + 1):
                st = _dot(k_ext, qx_ref[mp])
                s_ref[mp] = st
                tops.append(jnp.max(st, axis=0, keepdims=True))
        return tuple(tops)

    def softmax(m, st, top, offset):
        m_new = jnp.maximum(m, top + offset)
        alpha = jnp.exp2(m - m_new)
        p = jnp.exp2(st - (m_new - offset))
        return m_new, alpha, p.astype(BF16)

    ones_rows = jnp.ones((acc_ref.shape[1] - dh2, t), BF16)

    def values(i, alphas, p_of_map):
        k0 = pl.multiple_of(jnp.maximum(i, 0) * t, t)
        for hh in range(ATT_HEADS_PER_STEP):
            vt = jnp.concatenate([vt_ref[0, hh * dh2:(hh + 1) * dh2, pl.ds(k0, t)], ones_rows], axis=0)
            for mp in (2 * hh, 2 * hh + 1):
                acc_ref[mp] = alphas[mp] * acc_ref[mp] + _dot(vt, p_of_map(mp))

    def past_step(i, carry, s_cur, s_next, p_cur, p_prev):
        ms, alphas, tops = carry
        next_tops = scores(i + 1, s_next)
        values(i - 1, alphas, lambda mp: p_prev[mp])
        tile_offset = ((i - qi) * t).astype(F32)
        new_ms, new_alphas = [], []
        for mp in range(n_maps):
            m_new, alpha, p = softmax(ms[mp], s_cur[mp], tops[mp], slopes[mp // 2] * tile_offset)
            p_cur[mp] = p
            new_ms.append(m_new)
            new_alphas.append(alpha)
        return tuple(new_ms), tuple(new_alphas), next_tops

    def diagonal_step(carry, s_cur, p_prev):
        ms, alphas, _ = carry
        values(qi - 1, alphas, lambda mp: p_prev[mp])
        ps, new_alphas = [], []
        for mp in range(n_maps):
            two_a = 2.0 * slopes[mp // 2]
            blocks = []
            for c0 in range(0, t, LANES):
                lanes = slice(c0, c0 + LANES)
                own = slice(c0, c0 + LANES)
                near = s_cur[mp, own, lanes] + diag_ref[own, lanes] * two_a
                blocks.append(near if c0 == 0 else jnp.concatenate([s_cur[mp, 0:c0, lanes], near], axis=0))
            top = jnp.concatenate([jnp.max(b, axis=0, keepdims=True) for b in blocks], axis=1)
            m_new = jnp.maximum(ms[mp], top)
            p_blocks = []
            for c0, b in zip(range(0, t, LANES), blocks):
                p_blk = jnp.exp2(b - m_new[:, c0:c0 + LANES]).astype(BF16)
                if c0 + LANES < t:
                    p_blk = jnp.concatenate([p_blk, jnp.zeros((t - c0 - LANES, LANES), BF16)], axis=0)
                p_blocks.append(p_blk)
            ps.append(jnp.concatenate(p_blocks, axis=1))
            new_alphas.append(jnp.exp2(ms[mp] - m_new))
        values(qi, new_alphas, lambda mp: ps[mp])

    def pair(ii, carry):
        carry = past_step(2 * ii, carry, s_a, s_b, p_a, p_b)
        return past_step(2 * ii + 1, carry, s_b, s_a, p_b, p_a)

    first_tops = scores(0, s_a)
    init = ((jnp.full((1, t), NEG_BIG, F32),) * n_maps, (jnp.ones((1, t), F32),) * n_maps, first_tops)
    carry = lax.fori_loop(0, qi // 2, pair, init)

    @pl.when(qi % 2 == 1)
    def _():
        diagonal_step(past_step(qi - 1, carry, s_a, s_b, p_a, p_b), s_b, p_a)

    @pl.when(qi % 2 == 0)
    def _():
        diagonal_step(carry, s_a, p_b)

    lam = lam_ref[...]
    lam_full = (jnp.exp(jnp.sum(lam[0:1] * lam[1:2], axis=-1, keepdims=True))
                - jnp.exp(jnp.sum(lam[2:3] * lam[3:4], axis=-1, keepdims=True)) + lam_init)
    for hh in range(ATT_HEADS_PER_STEP):
        m0, m1 = 2 * hh, 2 * hh + 1
        ot = (acc_ref[m0, 0:dh2, :] / acc_ref[m0, dh2:dh2 + 1, :]
              - lam_full * (acc_ref[m1, 0:dh2, :] / acc_ref[m1, dh2:dh2 + 1, :]))
        ms = jnp.mean(ot * ot, axis=0, keepdims=True)
        yt = ot * lax.rsqrt(ms + NORM_EPS) * sub_ref[...] * (1.0 - lam_init)
        y_ref[0, :, hh * dh2:(hh + 1) * dh2] = yt.astype(BF16).T


def _flash(qt, k, vt, lam, subln_col, lam_init):
    bsz, seq, d = k.shape
    dh2 = 2 * DA_HEAD_DIM
    t = ATT_TILE
    hps = ATT_HEADS_PER_STEP
    s_buf = pltpu.VMEM((2 * hps, t, t), F32)
    p_buf = pltpu.VMEM((2 * hps, t, t), BF16)
    return pl.pallas_call(
        functools.partial(_flash_kernel, lam_init=lam_init),
        grid=(bsz, DA_HEADS // hps, seq // t),
        in_specs=[pl.BlockSpec((1, hps * dh2, t), lambda b, h, i: (b, h, i)),
                  pl.BlockSpec((1, seq, hps * dh2), lambda b, h, i: (b, 0, h), pipeline_mode=pl.Buffered(1)),
                  pl.BlockSpec((1, hps * dh2, seq), lambda b, h, i: (b, h, 0), pipeline_mode=pl.Buffered(1)),
                  _resident(lam.shape), _resident(subln_col.shape)],
        out_specs=pl.BlockSpec((1, t, hps * dh2), lambda b, h, i: (b, i, h)),
        out_shape=jax.ShapeDtypeStruct((bsz, seq, d), BF16),
        scratch_shapes=[pltpu.VMEM((2 * hps, 2 * dh2, t), BF16), pltpu.VMEM((t, dh2), BF16),
                        s_buf, s_buf, p_buf, p_buf, pltpu.VMEM((2 * hps, dh2 + BF16_ROWS, t), F32),
                        pltpu.VMEM((t, t), F32)],
        compiler_params=_params(3, ATT_VMEM_LIMIT),
        name="attn_flash",
    )(qt, k, vt, lam, subln_col)


def _proj_residual_kernel(h_ref, y_ref, gains_ref, wo_ref, o_ref):
    for r0 in range(0, h_ref.shape[0], PROJ_SUB_ROWS):
        rows = slice(r0, r0 + PROJ_SUB_ROWS)
        m = _dot(y_ref[rows, :], wo_ref[...])
        o_ref[rows, :] = h_ref[rows, :] + _rmsnorm(m, gains_ref[3:4, :])


def _proj_residual(h2, y2, gains, w_out):
    n, d = h2.shape
    tile = pl.BlockSpec((PROJ_ROWS, d), lambda i: (i, 0))
    return pl.pallas_call(
        _proj_residual_kernel,
        grid=(n // PROJ_ROWS,),
        in_specs=[tile, tile, _resident(gains.shape), _resident(w_out.shape)],
        out_specs=tile,
        out_shape=jax.ShapeDtypeStruct((n, d), F32),
        compiler_params=_params(1),
        name="attn_out",
    )(h2, y2, gains, w_out)


def _attn_layer(h, gains, w_in, lam, subln, w_out, layer):
    bsz, seq, d = h.shape
    lam_init = 0.8 - 0.6 * math.exp(-0.3 * layer)
    qt, k, vt = _qkv(h, gains, w_in)
    y = _flash(qt, k, vt, lam, subln.reshape(-1, 1), lam_init)
    out = _proj_residual(h.reshape(bsz * seq, d), y.reshape(bsz * seq, d), gains, w_out)
    return out.reshape(bsz, seq, d)


def _pool_kernel(h_ref, gains_ref, wi_ref, wg_ref, sc_ref, wo_ref, o_ref, carry_s, y_s):
    ts, d = h_ref.shape[1], h_ref.shape[2]
    gd = d // len(POOL_WINDOWS)
    halo = POOL_WINDOWS[-1]
    s_idx = pl.program_id(1)

    @pl.when(s_idx == 0)
    def _():
        carry_s[...] = jnp.zeros_like(carry_s)

    prev_tail = carry_s[...]
    for r0 in range(0, ts, PROJ_SUB_ROWS):
        rows = slice(r0, r0 + PROJ_SUB_ROWS)
        x = h_ref[0, rows, :]
        hn = _rmsnorm(x, gains_ref[2:3, :]).astype(BF16)
        u = _dot(hn, wi_ref[...])
        cur = jnp.concatenate([prev_tail, u], axis=0)
        prev_tail = u[PROJ_SUB_ROWS - halo:, :]
        pos1 = s_idx * ts + r0 + lax.broadcasted_iota(jnp.int32, (PROJ_SUB_ROWS, gd), 0) + 1
        for g, w in enumerate(POOL_WINDOWS):
            cur = cur + pltpu.roll(cur, w // 2, axis=0)
            count = jnp.minimum(pos1, w).astype(F32)
            pooled = cur[halo:, 0:gd] / count - u[:, g * gd:(g + 1) * gd]
            yg = _dot(pooled.astype(BF16), wg_ref[g]) * sc_ref[:, g * gd:(g + 1) * gd]
            y_s[rows, g * gd:(g + 1) * gd] = yg.astype(BF16)
            cur = cur[:, gd:]
        m = _dot(y_s[rows, :], wo_ref[...])
        o_ref[0, rows, :] = x + _rmsnorm(m, gains_ref[3:4, :])
    carry_s[...] = prev_tail


def _pool_layer(h, gains, w_in, w_group, scale, w_out):
    bsz, seq, d = h.shape
    ts = MIX_ROWS
    return pl.pallas_call(
        _pool_kernel,
        grid=(bsz, seq // ts),
        in_specs=[pl.BlockSpec((1, ts, d), lambda b, s: (b, s, 0)),
                  _resident(gains.shape), _resident(w_in.shape), _resident(w_group.shape),
                  _resident(scale.shape), _resident(w_out.shape)],
        out_specs=pl.BlockSpec((1, ts, d), lambda b, s: (b, s, 0)),
        out_shape=jax.ShapeDtypeStruct(h.shape, F32),
        scratch_shapes=[pltpu.VMEM((POOL_WINDOWS[-1], d), F32), pltpu.VMEM((ts, d), BF16)],
        compiler_params=_params(2),
        name="pool",
    )(h, gains, w_in, w_group, scale, w_out)


def kernel(x, norm_gains, ffn1_w_in, ffn1_w_out, ffn2_w_in, ffn2_w_out, hgrn_w_in, hgrn_gnorm,
           hgrn_w_out, hgrn_lb_raw, diff_w_in, diff_lambda, diff_subln, diff_w_out, pool_w_in,
           pool_w_group, pool_scale, pool_w_out):
    bsz, seq, d = x.shape
    depth = norm_gains.shape[0]
    n_mixers = 3
    bf = lambda w: w.astype(BF16)
    h = x
    for i in range(depth):
        kind, j = i % n_mixers, i // n_mixers
        gains = norm_gains[i]
        h = _ffn(h.reshape(bsz * seq, d), gains, bf(ffn1_w_in[i]), bf(ffn1_w_out[i]), 0, 1)
        h = h.reshape(bsz, seq, d)
        if kind == 0:
            h = _hgrn_layer(h, gains, bf(hgrn_w_in[j]), hgrn_gnorm[j][None, :], bf(hgrn_w_out[j]),
                            hgrn_lb_raw, i)
        elif kind == 1:
            h = _attn_layer(h, gains, bf(diff_w_in[j]), diff_lambda[j], diff_subln[j][None, :],
                            bf(diff_w_out[j]), i)
        else:
            h = _pool_layer(h, gains, bf(pool_w_in[j]), bf(pool_w_group[j]), pool_scale[j][None, :],
                            bf(pool_w_out[j]))
        h = _ffn(h.reshape(bsz * seq, d), gains, bf(ffn2_w_in[i]), bf(ffn2_w_out[i]), 4, 5)
        h = h.reshape(bsz, seq, d)
    return h
```
